```python
import math
import jax
import jax.numpy as jnp
from jax import lax
import numpy as np


D_MODEL = 1024
BATCH = 1
SEQ = 16384
DEPTH = 2

CHUNK = 64
Q_BLOCK = 128
N_MEM = 256
HEAD_DIM = 64
EPS = 1e-6
A_HEADS = D_MODEL // (2 * HEAD_DIM)
IDX_HEADS = 8
IDX_DIM = 64
TOPK_MAX = 256
B_VDIM = 2 * HEAD_DIM
B_HEADS = D_MODEL // (2 * B_VDIM)
C_HEADS = D_MODEL // HEAD_DIM
C_BAND = 9
REL_CLIP = 256
T5_BUCKETS = 32
T5_MAX_DIST = 128
T5_HEADS = A_HEADS + B_HEADS
M_HEADS = 4
M_DIM = 128
D_FF = 2816

kernel_name = 'hybrid_chunk_causal_encoder'


def even_split_sizes():
    return [A_HEADS * HEAD_DIM, A_HEADS * HEAD_DIM, A_HEADS * HEAD_DIM,
            IDX_HEADS * IDX_DIM, IDX_DIM, IDX_HEADS,
            B_HEADS * 2 * HEAD_DIM, B_HEADS * 2 * HEAD_DIM, B_HEADS * B_VDIM]


def rmsnorm(x, g):
    xf = x.astype(jnp.float32)
    y = xf * lax.rsqrt(jnp.mean(xf * xf, axis=-1, keepdims=True) + EPS)
    return (y * g.astype(jnp.float32)).astype(x.dtype)


def swiglu(x, wg, wu, wd):
    return (jax.nn.silu(x @ wg) * (x @ wu)) @ wd


def t5_bucket(rel):
    nb = T5_BUCKETS // 2
    max_exact = nb // 2
    offset = (rel < 0).astype(jnp.int32) * nb
    n = jnp.abs(rel)
    nf = jnp.maximum(n, 1).astype(jnp.float32)
    large = max_exact + (jnp.log(nf / max_exact) / math.log(T5_MAX_DIST / max_exact)
                         * (nb - max_exact)).astype(jnp.int32)
    large = jnp.minimum(large, nb - 1)
    return offset + jnp.where(n < max_exact, n, large)


def dsa_attention(q, k, v, iq, ik, iw, bias_table):
    bsz, seq, heads, dh = q.shape
    top_k = min(TOPK_MAX, seq // 4)
    key_chunk = jnp.arange(seq, dtype=jnp.int32) // CHUNK
    bias_table = bias_table.astype(jnp.float32)

    def block(i):
        start = i * Q_BLOCK
        qpos = start + jnp.arange(Q_BLOCK, dtype=jnp.int32)
        qchunk = qpos // CHUNK
        qb = lax.dynamic_slice_in_dim(q, start, Q_BLOCK, axis=1)
        iqb = lax.dynamic_slice_in_dim(iq, start, Q_BLOCK, axis=1)
        iwb = lax.dynamic_slice_in_dim(iw, start, Q_BLOCK, axis=1).astype(jnp.float32) * IDX_HEADS ** -0.5
        idx_logits = jnp.einsum('bqhd,bsd->bqhs', iqb, ik).astype(jnp.float32) * IDX_DIM ** -0.5
        score = jnp.einsum('bqh,bqhs->bqs', iwb, jax.nn.relu(idx_logits))
        admissible = key_chunk[None, :] <= qchunk[:, None]
        score = jnp.where(admissible[None], score, -jnp.inf)
        _, sel = lax.top_k(score, top_k)
        valid = (sel // CHUNK) <= qchunk[None, :, None]
        kg = jax.vmap(lambda kk, ii: kk[ii])(k, sel)
        vg = jax.vmap(lambda vv, ii: vv[ii])(v, sel)
        logits = jnp.einsum('bqhd,bqkhd->bhqk', qb, kg).astype(jnp.float32) * dh ** -0.5
        bias = bias_table[t5_bucket(qpos[None, :, None] - sel)]
        logits = logits + jnp.transpose(bias, (0, 3, 1, 2))
        logits = jnp.where(valid[:, None], logits, -jnp.inf)
        p = jax.nn.softmax(logits, axis=-1).astype(v.dtype)
        return jnp.einsum('bhqk,bqkhd->bqhd', p, vg)

    out = lax.map(block, jnp.arange(seq // Q_BLOCK, dtype=jnp.int32))
    return jnp.transpose(out, (1, 0, 2, 3, 4)).reshape(bsz, seq, heads * dh)


def diff_attention(q, k, v, lam, subln_g, bias_table):
    bsz, seq, heads, _, dh = q.shape
    pos = jnp.arange(seq, dtype=jnp.int32)
    key_chunk = pos // CHUNK
    bias_table = bias_table.astype(jnp.float32)

    def block(i):
        start = i * Q_BLOCK
        qpos = start + jnp.arange(Q_BLOCK, dtype=jnp.int32)
        qb = lax.dynamic_slice_in_dim(q, start, Q_BLOCK, axis=1)
        logits = jnp.einsum('bqhmd,bshmd->bhmqs', qb, k).astype(jnp.float32) * dh ** -0.5
        bias = bias_table[t5_bucket(qpos[:, None] - pos[None, :])]
        logits = logits + jnp.transpose(bias, (2, 0, 1))[None, :, None]
        mask = key_chunk[None, :] <= (qpos // CHUNK)[:, None]
        logits = jnp.where(mask, logits, -jnp.inf)
        p = jax.nn.softmax(logits, axis=-1)
        attn = (p[:, :, 0] - lam * p[:, :, 1]).astype(v.dtype)
        o = jnp.einsum('bhqs,bshe->bqhe', attn, v)
        return rmsnorm(o, subln_g)

    out = lax.map(block, jnp.arange(seq // Q_BLOCK, dtype=jnp.int32))
    return jnp.transpose(out, (1, 0, 2, 3, 4)).reshape(bsz, seq, heads * v.shape[-1])


def chunk_band_attention(q, k, v, rel_bias):
    bsz, seq, heads, dh = q.shape
    pad = (C_BAND - 1) * CHUNK
    band = C_BAND * CHUNK
    kp = jnp.pad(k, ((0, 0), (pad, 0), (0, 0), (0, 0)))
    vp = jnp.pad(v, ((0, 0), (pad, 0), (0, 0), (0, 0)))
    qoff = jnp.arange(CHUNK, dtype=jnp.int32)
    koff = jnp.arange(band, dtype=jnp.int32)
    rel = (qoff[:, None] + pad) - koff[None, :]
    rel_idx = jnp.clip(rel, -REL_CLIP, REL_CLIP) + REL_CLIP
    bias = jnp.transpose(rel_bias.astype(jnp.float32)[rel_idx], (2, 0, 1))

    def one_chunk(c):
        qc = lax.dynamic_slice_in_dim(q, c * CHUNK, CHUNK, axis=1)
        kc = lax.dynamic_slice_in_dim(kp, c * CHUNK, band, axis=1)
        vc = lax.dynamic_slice_in_dim(vp, c * CHUNK, band, axis=1)
        valid = (c * CHUNK - pad + koff) >= 0
        logits = jnp.einsum('bqhd,bkhd->bhqk', qc, kc).astype(jnp.float32) * dh ** -0.5 + bias[None]
        logits = jnp.where(valid, logits, -jnp.inf)
        p = jax.nn.softmax(logits, axis=-1).astype(vc.dtype)
        return jnp.einsum('bhqk,bkhd->bqhd', p, vc)

    out = lax.map(one_chunk, jnp.arange(seq // CHUNK, dtype=jnp.int32))
    return jnp.transpose(out, (1, 0, 2, 3, 4)).reshape(bsz, seq, heads * dh)


def even_mixer(h, w_in, a_qg, a_kg, idx_kg, b_qg, b_kg, lq1, lk1, lq2, lk2, b_subln, w_out, t5_bias, layer_idx):
    bsz, seq, _ = h.shape
    cuts = [int(c) for c in np.cumsum(even_split_sizes())[:-1]]
    aq, ak, av, iq, ik, iw, bq, bk, bv = jnp.split(h @ w_in, cuts, axis=-1)
    aq = rmsnorm(aq.reshape(bsz, seq, A_HEADS, HEAD_DIM), a_qg)
    ak = rmsnorm(ak.reshape(bsz, seq, A_HEADS, HEAD_DIM), a_kg)
    av = av.reshape(bsz, seq, A_HEADS, HEAD_DIM)
    iq = iq.reshape(bsz, seq, IDX_HEADS, IDX_DIM)
    ik = rmsnorm(ik, idx_kg)
    out_a = dsa_attention(aq, ak, av, iq, ik, iw, t5_bias[:, :A_HEADS])
    bq = rmsnorm(bq.reshape(bsz, seq, B_HEADS, 2, HEAD_DIM), b_qg)
    bk = rmsnorm(bk.reshape(bsz, seq, B_HEADS, 2, HEAD_DIM), b_kg)
    bv = bv.reshape(bsz, seq, B_HEADS, B_VDIM)
    lambda_init = 0.8 - 0.6 * math.exp(-0.3 * layer_idx)
    lam = (jnp.exp(jnp.sum(lq1.astype(jnp.float32) * lk1.astype(jnp.float32)))
           - jnp.exp(jnp.sum(lq2.astype(jnp.float32) * lk2.astype(jnp.float32))) + lambda_init)
    out_b = diff_attention(bq, bk, bv, lam, b_subln, t5_bias[:, A_HEADS:]) * (1.0 - lambda_init)
    return jnp.concatenate([out_a, out_b], axis=-1) @ w_out


def odd_mixer(h, w_in, c_qg, c_kg, rel_bias, w_out):
    bsz, seq, _ = h.shape
    q, k, v = jnp.split(h @ w_in, 3, axis=-1)
    q = rmsnorm(q.reshape(bsz, seq, C_HEADS, HEAD_DIM), c_qg)
    k = rmsnorm(k.reshape(bsz, seq, C_HEADS, HEAD_DIM), c_kg)
    v = v.reshape(bsz, seq, C_HEADS, HEAD_DIM)
    return chunk_band_attention(q, k, v, rel_bias) @ w_out


def memory_xattn(h, m, wq, wkv, qg, kg, wo):
    bsz, seq, _ = h.shape
    q = rmsnorm((h @ wq).reshape(bsz, seq, M_HEADS, M_DIM), qg)
    k, v = jnp.split(m @ wkv, 2, axis=-1)
    k = rmsnorm(k.reshape(bsz, -1, M_HEADS, M_DIM), kg)
    v = v.reshape(bsz, -1, M_HEADS, M_DIM)
    logits = jnp.einsum('bshd,bmhd->bhsm', q, k).astype(jnp.float32) * M_DIM ** -0.5
    p = jax.nn.softmax(logits, axis=-1).astype(v.dtype)
    o = jnp.einsum('bhsm,bmhd->bshd', p, v).reshape(bsz, seq, M_HEADS * M_DIM)
    return o @ wo


def setup_inputs(seed: int = 0) -> dict:
    key = jax.random.key(seed)
    counter = [0]

    def nk():
        counter[0] += 1
        return jax.random.fold_in(key, counter[0])

    def dense(fan_in, fan_out):
        return jax.random.normal(nk(), (fan_in, fan_out), jnp.float32) * fan_in ** -0.5

    def gain(n):
        return 1.0 + 0.02 * jax.random.normal(nk(), (n,), jnp.float32)

    def small(shape, scale):
        return scale * jax.random.normal(nk(), shape, jnp.float32)

    p = {}
    p['x'] = jax.random.normal(nk(), (BATCH, SEQ, D_MODEL), jnp.float32)
    p['mem'] = jax.random.normal(nk(), (BATCH, N_MEM, D_MODEL), jnp.float32)
    p['t5_bias'] = small((T5_BUCKETS, T5_HEADS), 0.2)
    for layer in range(DEPTH):
        pre = 'l%d_' % layer
        p[pre + 'ffn1_norm'] = gain(D_MODEL)
        p[pre + 'ffn1_wg'] = dense(D_MODEL, D_FF)
        p[pre + 'ffn1_wu'] = dense(D_MODEL, D_FF)
        p[pre + 'ffn1_wd'] = dense(D_FF, D_MODEL)
        p[pre + 'mix_norm'] = gain(D_MODEL)
        if layer % 2 == 0:
            p[pre + 'w_in'] = dense(D_MODEL, sum(even_split_sizes()))
            p[pre + 'a_q_norm'] = gain(HEAD_DIM)
            p[pre + 'a_k_norm'] = gain(HEAD_DIM)
            p[pre + 'idx_k_norm'] = gain(IDX_DIM)
            p[pre + 'b_q_norm'] = gain(HEAD_DIM)
            p[pre + 'b_k_norm'] = gain(HEAD_DIM)
            p[pre + 'b_lq1'] = small((HEAD_DIM,), 0.1)
            p[pre + 'b_lk1'] = small((HEAD_DIM,), 0.1)
            p[pre + 'b_lq2'] = small((HEAD_DIM,), 0.1)
            p[pre + 'b_lk2'] = small((HEAD_DIM,), 0.1)
            p[pre + 'b_subln'] = gain(B_VDIM)
            p[pre + 'w_out'] = dense(A_HEADS * HEAD_DIM + B_HEADS * B_VDIM, D_MODEL)
        else:
            p[pre + 'w_in'] = dense(D_MODEL, 3 * C_HEADS * HEAD_DIM)
            p[pre + 'c_q_norm'] = gain(HEAD_DIM)
            p[pre + 'c_k_norm'] = gain(HEAD_DIM)
            p[pre + 'c_rel_bias'] = small((2 * REL_CLIP + 1, C_HEADS), 0.2)
            p[pre + 'w_out'] = dense(C_HEADS * HEAD_DIM, D_MODEL)
        p[pre + 'mem_norm'] = gain(D_MODEL)
        p[pre + 'mem_src_norm'] = gain(D_MODEL)
        p[pre + 'mem_wq'] = dense(D_MODEL, M_HEADS * M_DIM)
        p[pre + 'mem_wkv'] = dense(D_MODEL, 2 * M_HEADS * M_DIM)
        p[pre + 'mem_q_norm'] = gain(M_DIM)
        p[pre + 'mem_k_norm'] = gain(M_DIM)
        p[pre + 'mem_wo'] = dense(M_HEADS * M_DIM, D_MODEL)
        p[pre + 'ffn2_norm'] = gain(D_MODEL)
        p[pre + 'ffn2_wg'] = dense(D_MODEL, D_FF)
        p[pre + 'ffn2_wu'] = dense(D_MODEL, D_FF)
        p[pre + 'ffn2_wd'] = dense(D_FF, D_MODEL)
    return p


def reference(x, mem, t5_bias,
              l0_ffn1_norm, l0_ffn1_wg, l0_ffn1_wu, l0_ffn1_wd,
              l0_mix_norm, l0_w_in, l0_a_q_norm, l0_a_k_norm, l0_idx_k_norm,
              l0_b_q_norm, l0_b_k_norm, l0_b_lq1, l0_b_lk1, l0_b_lq2, l0_b_lk2, l0_b_subln, l0_w_out,
              l0_mem_norm, l0_mem_src_norm, l0_mem_wq, l0_mem_wkv, l0_mem_q_norm, l0_mem_k_norm, l0_mem_wo,
              l0_ffn2_norm, l0_ffn2_wg, l0_ffn2_wu, l0_ffn2_wd,
              l1_ffn1_norm, l1_ffn1_wg, l1_ffn1_wu, l1_ffn1_wd,
              l1_mix_norm, l1_w_in, l1_c_q_norm, l1_c_k_norm, l1_c_rel_bias, l1_w_out,
              l1_mem_norm, l1_mem_src_norm, l1_mem_wq, l1_mem_wkv, l1_mem_q_norm, l1_mem_k_norm, l1_mem_wo,
              l1_ffn2_norm, l1_ffn2_wg, l1_ffn2_wu, l1_ffn2_wd):
    ffn1 = [(l0_ffn1_norm, l0_ffn1_wg, l0_ffn1_wu, l0_ffn1_wd),
            (l1_ffn1_norm, l1_ffn1_wg, l1_ffn1_wu, l1_ffn1_wd)]
    mix = [(l0_mix_norm, (l0_w_in, l0_a_q_norm, l0_a_k_norm, l0_idx_k_norm, l0_b_q_norm, l0_b_k_norm,
                          l0_b_lq1, l0_b_lk1, l0_b_lq2, l0_b_lk2, l0_b_subln, l0_w_out)),
           (l1_mix_norm, (l1_w_in, l1_c_q_norm, l1_c_k_norm, l1_c_rel_bias, l1_w_out))]
    memx = [(l0_mem_norm, l0_mem_src_norm, l0_mem_wq, l0_mem_wkv, l0_mem_q_norm, l0_mem_k_norm, l0_mem_wo),
            (l1_mem_norm, l1_mem_src_norm, l1_mem_wq, l1_mem_wkv, l1_mem_q_norm, l1_mem_k_norm, l1_mem_wo)]
    ffn2 = [(l0_ffn2_norm, l0_ffn2_wg, l0_ffn2_wu, l0_ffn2_wd),
            (l1_ffn2_norm, l1_ffn2_wg, l1_ffn2_wu, l1_ffn2_wd)]
    h = x
    for layer in range(DEPTH):
        g, wg, wu, wd = ffn1[layer]
        h = h + 0.5 * swiglu(rmsnorm(h, g), wg, wu, wd)
        mix_g, mix_params = mix[layer]
        if layer % 2 == 0:
            h = h + even_mixer(rmsnorm(h, mix_g), *mix_params, t5_bias, layer)
        else:
            h = h + odd_mixer(rmsnorm(h, mix_g), *mix_params)
        mg, sg, wq, wkv, qg, kg, wo = memx[layer]
        h = h + memory_xattn(rmsnorm(h, mg), rmsnorm(mem, sg), wq, wkv, qg, kg, wo)
        g, wg, wu, wd = ffn2[layer]
        h = h + 0.5 * swiglu(rmsnorm(h, g), wg, wu, wd)
    return h
```

```python
import functools
import math

import jax
import jax.numpy as jnp
import numpy as np
from jax import lax
from jax.experimental import pallas as pl
from jax.experimental.pallas import tpu as pltpu

D_MODEL = 1024
CHUNK = 64
HEAD_DIM = 64
EPS = 1e-6
A_HEADS = 8
IDX_HEADS = 8
IDX_DIM = 64
TOPK_MAX = 256
B_VDIM = 128
B_HEADS = 4
C_HEADS = 16
C_BAND = 9
REL_CLIP = 256
T5_BUCKETS = 32
T5_MAX_DIST = 128
M_HEADS = 4
M_DIM = 128
D_FF = 2816

F32 = jnp.float32
BF16 = jnp.bfloat16
I32 = jnp.int32

NEG = -1e30
INT_MIN = -(2 ** 31)
VMEM_LIMIT = 56 * 1024 * 1024

ROW_TILE = 512
FF_CHUNK = 256
TQ = 256
TK = 256
BAND_T = 512


def _cparams(n_axes):
    return pltpu.CompilerParams(dimension_semantics=("arbitrary",) * n_axes,
                                vmem_limit_bytes=VMEM_LIMIT)


def _dot(a, b):
    return jnp.dot(a, b, preferred_element_type=F32)


def _rms_rows(x, g):
    ms = jnp.mean(x * x, axis=-1, keepdims=True)
    return x * lax.rsqrt(ms + EPS) * g


def _ffn_body(h_ref, g_ref, wg_ref, wu_ref, wd_ref, o_ref, xn_ref, acc_ref):
    x = h_ref[...]
    xn_ref[...] = _rms_rows(x, g_ref[...]).astype(BF16)
    acc_ref[...] = jnp.zeros_like(acc_ref)

    def chunk(c, carry):
        xn = xn_ref[...]
        gate = _dot(xn, wg_ref[c])
        up = _dot(xn, wu_ref[c])
        act = (gate * jax.nn.sigmoid(gate) * up).astype(BF16)
        acc_ref[...] += _dot(act, wd_ref[c])
        return carry

    lax.fori_loop(0, wg_ref.shape[0], chunk, 0)
    o_ref[...] = x + 0.5 * acc_ref[...]


def _ffn(h, g, wg, wu, wd):
    s, d = h.shape
    ff = wg.shape[1]
    nc = ff // FF_CHUNK
    wg3 = wg.astype(BF16).reshape(d, nc, FF_CHUNK).transpose(1, 0, 2)
    wu3 = wu.astype(BF16).reshape(d, nc, FF_CHUNK).transpose(1, 0, 2)
    wd3 = wd.astype(BF16).reshape(nc, FF_CHUNK, d)
    tm = min(ROW_TILE, s)
    const3 = lambda i: (0, 0, 0)
    return pl.pallas_call(
        _ffn_body,
        grid=(s // tm,),
        in_specs=[
            pl.BlockSpec((tm, d), lambda i: (i, 0)),
            pl.BlockSpec((1, d), lambda i: (0, 0)),
            pl.BlockSpec((nc, d, FF_CHUNK), const3, pipeline_mode=pl.Buffered(1)),
            pl.BlockSpec((nc, d, FF_CHUNK), const3, pipeline_mode=pl.Buffered(1)),
            pl.BlockSpec((nc, FF_CHUNK, d), const3, pipeline_mode=pl.Buffered(1)),
        ],
        out_specs=pl.BlockSpec((tm, d), lambda i: (i, 0)),
        out_shape=jax.ShapeDtypeStruct((s, d), F32),
        scratch_shapes=[pltpu.VMEM((tm, d), BF16), pltpu.VMEM((tm, d), F32)],
        compiler_params=_cparams(1),
        name="ffn",
    )(h, g.reshape(1, d), wg3, wu3, wd3)


class _Seg:
    def __init__(self, w, head_dim=None, gain=None, scale=1.0, layout="fm",
                 dtype=BF16, tile=TK, pad=0):
        self.w = w
        self.head_dim = head_dim
        self.gain = gain
        self.scale = scale
        self.layout = layout
        self.dtype = dtype
        self.tile = tile
        self.pad = pad


def _proj_body(segs, x_ref, g_ref, *refs):
    n = len(segs)
    w_refs = refs[:n]
    gains = [r for r in refs[n:2 * n]]
    outs = refs[2 * n:]
    xn = _rms_rows(x_ref[...], g_ref[...]).astype(BF16)
    tm = xn.shape[0]
    for seg, w_ref, gain_ref, o_ref in zip(segs, w_refs, gains, outs):
        yt = lax.dot_general(w_ref[...], xn, (((1,), (1,)), ((), ())),
                             preferred_element_type=F32)
        f = yt.shape[0]
        if seg.head_dim is not None:
            hd = seg.head_dim
            y3 = yt.reshape(f // hd, hd, tm)
            ms = jnp.mean(y3 * y3, axis=1, keepdims=True)
            y3 = y3 * lax.rsqrt(ms + EPS) * gain_ref[...][None]
            yt = y3.reshape(f, tm)
        if seg.scale != 1.0:
            yt = yt * seg.scale
        if seg.pad:
            yt = jnp.concatenate([yt, jnp.zeros((seg.pad, tm), F32)], axis=0)
        if seg.layout == "fm":
            o_ref[...] = yt.astype(seg.dtype)
        elif seg.layout == "tm":
            o_ref[...] = yt.T.astype(seg.dtype)
        else:
            for t in range(tm // seg.tile):
                o_ref[t] = yt[:, t * seg.tile:(t + 1) * seg.tile].astype(seg.dtype)


def _project(x, g, segs):
    s, d = x.shape
    tm = min(ROW_TILE, s)
    in_specs = [pl.BlockSpec((tm, d), lambda i: (i, 0)),
                pl.BlockSpec((1, d), lambda i: (0, 0))]
    args = [x, g.reshape(1, d)]
    for seg in segs:
        wt = seg.w.T.astype(BF16)
        args.append(wt)
        in_specs.append(pl.BlockSpec(wt.shape, lambda i: (0, 0)))
    for seg in segs:
        hd = seg.head_dim or 8
        gain = seg.gain if seg.gain is not None else jnp.ones((hd,), F32)
        args.append(gain.astype(F32).reshape(hd, 1))
        in_specs.append(pl.BlockSpec((hd, 1), lambda i: (0, 0)))
    out_shapes, out_specs = [], []
    for seg in segs:
        f = seg.w.shape[1] + seg.pad
        if seg.layout == "fm":
            out_shapes.append(jax.ShapeDtypeStruct((f, s), seg.dtype))
            out_specs.append(pl.BlockSpec((f, tm), lambda i: (0, i)))
        elif seg.layout == "tm":
            out_shapes.append(jax.ShapeDtypeStruct((s, f), seg.dtype))
            out_specs.append(pl.BlockSpec((tm, f), lambda i: (i, 0)))
        else:
            nt = tm // seg.tile
            out_shapes.append(jax.ShapeDtypeStruct((s // seg.tile, f, seg.tile), seg.dtype))
            out_specs.append(pl.BlockSpec((nt, f, seg.tile), lambda i: (i, 0, 0)))
    return pl.pallas_call(
        functools.partial(_proj_body, segs),
        grid=(s // tm,),
        in_specs=in_specs,
        out_specs=out_specs,
        out_shape=out_shapes,
        compiler_params=_cparams(1),
        name="project",
    )(*args)


def _outproj_body(n, h_ref, *refs):
    acc = h_ref[...]
    for ot_ref, w_ref in zip(refs[:n], refs[n:2 * n]):
        acc = acc + _dot(ot_ref[...].T.astype(BF16), w_ref[...])
    refs[2 * n][...] = acc


def _outproj(h, ots, ws):
    s, d = h.shape
    tm = min(ROW_TILE, s)
    n = len(ots)
    in_specs = [pl.BlockSpec((tm, d), lambda i: (i, 0))]
    in_specs += [pl.BlockSpec((ot.shape[0], tm), lambda i: (0, i)) for ot in ots]
    in_specs += [pl.BlockSpec(w.shape, lambda i: (0, 0)) for w in ws]
    return pl.pallas_call(
        functools.partial(_outproj_body, n),
        grid=(s // tm,),
        in_specs=in_specs,
        out_specs=pl.BlockSpec((tm, d), lambda i: (i, 0)),
        out_shape=jax.ShapeDtypeStruct((s, d), F32),
        compiler_params=_cparams(1),
        name="outproj",
    )(h, *ots, *[w.astype(BF16) for w in ws])


def _softmax_step(s, vt, m_ref, l_ref, acc_ref, idx):
    m_old = m_ref[idx]
    m_new = jnp.maximum(m_old, jnp.max(s, axis=0, keepdims=True))
    alpha = jnp.exp(m_old - m_new)
    p = jnp.exp(s - m_new)
    l_ref[idx] = alpha * l_ref[idx] + jnp.sum(p, axis=0, keepdims=True)
    acc_ref[idx] = alpha * acc_ref[idx] + _dot(vt, p.astype(BF16))
    m_ref[idx] = m_new


def _init_state(m_ref, l_ref, acc_ref):
    m_ref[...] = jnp.full(m_ref.shape, NEG, F32)
    l_ref[...] = jnp.zeros_like(l_ref)
    acc_ref[...] = jnp.zeros_like(acc_ref)


def _half_masked(qpair):
    row = lax.broadcasted_iota(I32, qpair.shape, 0)
    zero = jnp.zeros_like(qpair)
    return jnp.where(row < HEAD_DIM, qpair, zero), jnp.where(row >= HEAD_DIM, qpair, zero)


def _t5_bucket(rel):
    nb = T5_BUCKETS // 2
    max_exact = nb // 2
    offset = (rel < 0).astype(jnp.int32) * nb
    n = jnp.abs(rel)
    nf = jnp.maximum(n, 1).astype(jnp.float32)
    large = max_exact + (jnp.log(nf / max_exact) / math.log(T5_MAX_DIST / max_exact)
                         * (nb - max_exact)).astype(jnp.int32)
    large = jnp.minimum(large, nb - 1)
    return offset + jnp.where(n < max_exact, n, large)


def _t5_bias_tiles(table):
    sk = jnp.arange(TK, dtype=jnp.int32)[:, None]
    tq = jnp.arange(TQ, dtype=jnp.int32)[None, :]
    tiles = []
    for dist in range(3):
        rel = dist * TK + tq - sk
        b = table.astype(F32)[_t5_bucket(rel)]
        if dist == 0:
            b = jnp.where(((sk // CHUNK) <= (tq // CHUNK))[:, :, None], b, NEG)
        tiles.append(jnp.transpose(b, (2, 0, 1)))
    return jnp.stack(tiles, axis=1)


def _sortable_key(x):
    b = lax.bitcast_convert_type(x, I32)
    return jnp.where(b < 0, INT_MIN - b, b)


def _dsa_body(top_k, iq_ref, iw_ref, ik_ref, q_ref, k_ref, vt_ref, bias_ref, o_ref,
              keys_ref, thr_ref, m_ref, l_ref, acc_ref):
    i = pl.program_id(0)
    g = pl.program_id(1)
    n_tiles = i + 1

    @pl.when(g == 0)
    def _select():
        w_all = iw_ref[...]
        zpad = jnp.zeros((IDX_DIM, TQ), BF16)
        qz = [jnp.concatenate([iq_ref[h * IDX_DIM:(h + 1) * IDX_DIM, :], zpad], axis=0)
              for h in range(IDX_HEADS)]

        def score_tile(j, carry):
            k0 = pl.multiple_of(j * TK, TK)
            ikt = ik_ref[pl.ds(k0, TK), :]
            sc = jnp.zeros((TK, TQ), F32)
            for h in range(IDX_HEADS):
                sc = sc + w_all[h:h + 1, :] * jnp.maximum(_dot(ikt, qz[h]), 0.0)
            keys_ref[pl.ds(k0, TK), :] = _sortable_key(sc)
            return carry

        lax.fori_loop(0, n_tiles, score_tile, 0)

        d0 = pl.multiple_of(i * TK, TK)
        sk = lax.broadcasted_iota(I32, (TK, TQ), 0) // CHUNK
        tq = lax.broadcasted_iota(I32, (TK, TQ), 1) // CHUNK
        keys_ref[pl.ds(d0, TK), :] = jnp.where(sk <= tq, keys_ref[pl.ds(d0, TK), :], INT_MIN)

        def count_ge(cand):
            def body(j, acc):
                k0 = pl.multiple_of(j * TK, TK)
                ge = (keys_ref[pl.ds(k0, TK), :] >= cand).astype(I32)
                return acc + jnp.sum(ge.reshape(TK // 8, 8, TQ), axis=0)
            acc = lax.fori_loop(0, n_tiles, body, jnp.zeros((8, TQ), I32))
            return jnp.sum(acc, axis=0, keepdims=True)

        thr = jnp.where(count_ge(jnp.zeros((1, TQ), I32)) >= top_k, 0, INT_MIN).astype(I32)

        def bit_step(b, thr):
            cand = thr + lax.shift_left(jnp.int32(1), 30 - b)
            return jnp.where(count_ge(cand) >= top_k, cand, thr)

        thr = lax.fori_loop(0, 31, bit_step, thr)
        thr_ref[...] = jnp.maximum(thr, INT_MIN + 1)

    _init_state(m_ref, l_ref, acc_ref)
    q_halves = _half_masked(q_ref[...])
    thr = thr_ref[...]

    def attn_tile(j, carry):
        k0 = pl.multiple_of(j * TK, TK)
        kt = k_ref[pl.ds(k0, TK), :]
        vt = vt_ref[j]
        sel = keys_ref[pl.ds(k0, TK), :] >= thr
        dist = jnp.minimum(i - j, 2)
        for a in range(2):
            s = _dot(kt, q_halves[a]) + bias_ref[a, dist]
            s = jnp.where(sel, s, NEG)
            _softmax_step(s, vt[a * HEAD_DIM:(a + 1) * HEAD_DIM, :], m_ref, l_ref, acc_ref, a)
        return carry

    lax.fori_loop(0, n_tiles, attn_tile, 0)
    for a in range(2):
        o_ref[a * HEAD_DIM:(a + 1) * HEAD_DIM, :] = acc_ref[a] / l_ref[a]


def _dsa_attention(iq_t, iw_t, ik, q_t, k, v_t, bias, top_k):
    s = k.shape[0]
    n_pairs = A_HEADS // 2
    return pl.pallas_call(
        functools.partial(_dsa_body, top_k),
        grid=(s // TQ, n_pairs),
        in_specs=[
            pl.BlockSpec((IDX_HEADS * IDX_DIM, TQ), lambda i, g: (0, i)),
            pl.BlockSpec((IDX_HEADS, TQ), lambda i, g: (0, i)),
            pl.BlockSpec((s, 128), lambda i, g: (0, 0)),
            pl.BlockSpec((128, TQ), lambda i, g: (g, i)),
            pl.BlockSpec((s, 128), lambda i, g: (0, g)),
            pl.BlockSpec((s // TK, 128, TK), lambda i, g: (0, g, 0)),
            pl.BlockSpec((2, 3, TK, TQ), lambda i, g: (g, 0, 0, 0)),
        ],
        out_specs=pl.BlockSpec((128, TQ), lambda i, g: (g, i)),
        out_shape=jax.ShapeDtypeStruct((A_HEADS * HEAD_DIM, s), F32),
        scratch_shapes=[
            pltpu.VMEM((s, TQ), I32),
            pltpu.VMEM((1, TQ), I32),
            pltpu.VMEM((2, 1, TQ), F32),
            pltpu.VMEM((2, 1, TQ), F32),
            pltpu.VMEM((2, HEAD_DIM, TQ), F32),
        ],
        compiler_params=_cparams(2),
        name="dsa_attention",
    )(iq_t, iw_t, ik, q_t, k, v_t, bias)


def _diff_body(lambda_init, q_ref, k_ref, vt_ref, bias_ref, lam_ref, subln_ref, o_ref,
               m_ref, l_ref, acc_ref):
    i = pl.program_id(0)
    _init_state(m_ref, l_ref, acc_ref)
    q_halves = _half_masked(q_ref[...])

    def attn_tile(j, carry):
        k0 = pl.multiple_of(j * TK, TK)
        kt = k_ref[pl.ds(k0, TK), :]
        vt = vt_ref[j]
        bias = bias_ref[0, jnp.minimum(i - j, 2)]
        for a in range(2):
            _softmax_step(_dot(kt, q_halves[a]) + bias, vt, m_ref, l_ref, acc_ref, a)
        return carry

    lax.fori_loop(0, i + 1, attn_tile, 0)

    lq1, lk1, lq2, lk2 = (lam_ref[r:r + 1, :] for r in range(4))
    lam = (jnp.exp(jnp.sum(lq1 * lk1, keepdims=True)) - jnp.exp(jnp.sum(lq2 * lk2, keepdims=True))
           + lambda_init)
    o = acc_ref[0] / l_ref[0] - lam * (acc_ref[1] / l_ref[1])
    ms = jnp.mean(o * o, axis=0, keepdims=True)
    o_ref[...] = o * lax.rsqrt(ms + EPS) * subln_ref[...] * (1.0 - lambda_init)


def _diff_attention(q_t, k, v_t, bias, lam_rows, subln, lambda_init):
    s = k.shape[0]
    return pl.pallas_call(
        functools.partial(_diff_body, lambda_init),
        grid=(s // TQ, B_HEADS),
        in_specs=[
            pl.BlockSpec((128, TQ), lambda i, h: (h, i)),
            pl.BlockSpec((s, 128), lambda i, h: (0, h)),
            pl.BlockSpec((s // TK, B_VDIM, TK), lambda i, h: (0, h, 0)),
            pl.BlockSpec((1, 3, TK, TQ), lambda i, h: (h, 0, 0, 0)),
            pl.BlockSpec((4, HEAD_DIM), lambda i, h: (0, 0)),
            pl.BlockSpec((B_VDIM, 1), lambda i, h: (0, 0)),
        ],
        out_specs=pl.BlockSpec((B_VDIM, TQ), lambda i, h: (h, i)),
        out_shape=jax.ShapeDtypeStruct((B_HEADS * B_VDIM, s), F32),
        scratch_shapes=[
            pltpu.VMEM((2, 1, TQ), F32),
            pltpu.VMEM((2, 1, TQ), F32),
            pltpu.VMEM((2, B_VDIM, TQ), F32),
        ],
        compiler_params=_cparams(2),
        name="diff_attention",
    )(q_t, k, v_t, bias, lam_rows, subln.astype(F32).reshape(B_VDIM, 1))


def _band_bias_tiles(rel_bias):
    sk = jnp.arange(BAND_T, dtype=jnp.int32)[:, None]
    tq = jnp.arange(BAND_T, dtype=jnp.int32)[None, :]
    table = rel_bias.astype(F32)
    tiles = []
    for blk in range(2):
        rel = (1 - blk) * BAND_T + tq - sk
        b = table[jnp.clip(rel, -REL_CLIP, REL_CLIP) + REL_CLIP]
        kc = sk // CHUNK + blk * (BAND_T // CHUNK)
        qc = tq // CHUNK + BAND_T // CHUNK
        ok = (kc <= qc) & (kc >= qc - (C_BAND - 1))
        b = jnp.where(ok[:, :, None], b, NEG)
        tiles.append(jnp.transpose(b, (2, 0, 1)))
    return jnp.stack(tiles, axis=1)


def _band_body(q_ref, kp_ref, kc_ref, vp_ref, vc_ref, bias_ref, o_ref, m_ref, l_ref, acc_ref):
    i = pl.program_id(1)
    _init_state(m_ref, l_ref, acc_ref)
    q_halves = _half_masked(q_ref[...])

    @pl.when(i > 0)
    def _prev():
        kt = kp_ref[...]
        vt = vp_ref[0]
        for a in range(2):
            s = _dot(kt, q_halves[a]) + bias_ref[a, 0]
            _softmax_step(s, vt[a * HEAD_DIM:(a + 1) * HEAD_DIM, :], m_ref, l_ref, acc_ref, a)

    kt = kc_ref[...]
    vt = vc_ref[0]
    for a in range(2):
        s = _dot(kt, q_halves[a]) + bias_ref[a, 1]
        _softmax_step(s, vt[a * HEAD_DIM:(a + 1) * HEAD_DIM, :], m_ref, l_ref, acc_ref, a)
        o_ref[a * HEAD_DIM:(a + 1) * HEAD_DIM, :] = acc_ref[a] / l_ref[a]


def _band_attention(q_t, k, v_t, bias):
    s = k.shape[0]
    t = BAND_T
    prev = lambda i: jnp.maximum(i - 1, 0)
    return pl.pallas_call(
        _band_body,
        grid=(C_HEADS // 2, s // t),
        in_specs=[
            pl.BlockSpec((128, t), lambda p, i: (p, i)),
            pl.BlockSpec((t, 128), lambda p, i: (prev(i), p)),
            pl.BlockSpec((t, 128), lambda p, i: (i, p)),
            pl.BlockSpec((1, 128, t), lambda p, i: (prev(i), p, 0)),
            pl.BlockSpec((1, 128, t), lambda p, i: (i, p, 0)),
            pl.BlockSpec((2, 2, t, t), lambda p, i: (p, 0, 0, 0)),
        ],
        out_specs=pl.BlockSpec((128, t), lambda p, i: (p, i)),
        out_shape=jax.ShapeDtypeStruct((C_HEADS * HEAD_DIM, s), F32),
        scratch_shapes=[
            pltpu.VMEM((2, 1, t), F32),
            pltpu.VMEM((2, 1, t), F32),
            pltpu.VMEM((2, HEAD_DIM, t), F32),
        ],
        compiler_params=_cparams(2),
        name="band_attention",
    )(q_t, k, k, v_t, v_t, bias)


def _mem_body(q_ref, k_ref, vt_ref, o_ref):
    for h in range(M_HEADS):
        rows = slice(h * M_DIM, (h + 1) * M_DIM)
        s = _dot(k_ref[:, rows], q_ref[rows, :])
        p = jnp.exp(s - jnp.max(s, axis=0, keepdims=True))
        denom = jnp.sum(p, axis=0, keepdims=True)
        o_ref[rows, :] = _dot(vt_ref[0, rows, :], p.astype(BF16)) / denom


def _mem_attention(q_t, k, v_t):
    f, s = q_t.shape
    tq = min(ROW_TILE, s)
    return pl.pallas_call(
        _mem_body,
        grid=(s // tq,),
        in_specs=[
            pl.BlockSpec((f, tq), lambda i: (0, i)),
            pl.BlockSpec(k.shape, lambda i: (0, 0)),
            pl.BlockSpec(v_t.shape, lambda i: (0, 0, 0)),
        ],
        out_specs=pl.BlockSpec((f, tq), lambda i: (0, i)),
        out_shape=jax.ShapeDtypeStruct((f, s), F32),
        compiler_params=_cparams(1),
        name="mem_attention",
    )(q_t, k, v_t)


def _even_mixer(h, mix_g, w_in, a_qg, a_kg, idx_kg, b_qg, b_kg, lq1, lk1, lq2, lk2, b_subln,
                w_out, t5_bias, layer_idx):
    s = h.shape[0]
    sizes = [A_HEADS * HEAD_DIM] * 3 + [IDX_HEADS * IDX_DIM, IDX_DIM, IDX_HEADS] + \
            [B_HEADS * 2 * HEAD_DIM] * 2 + [B_HEADS * B_VDIM]
    cuts = np.cumsum([0] + sizes)
    w = [w_in[:, cuts[n]:cuts[n + 1]] for n in range(len(sizes))]
    qk_scale = HEAD_DIM ** -0.5
    segs = [
        _Seg(w[0], HEAD_DIM, a_qg, qk_scale, "fm"),
        _Seg(w[1], HEAD_DIM, a_kg, 1.0, "tm"),
        _Seg(w[2], layout="vt"),
        _Seg(w[3], scale=IDX_DIM ** -0.5, layout="fm"),
        _Seg(w[4], IDX_DIM, idx_kg, 1.0, "tm", pad=IDX_DIM),
        _Seg(w[5], scale=IDX_HEADS ** -0.5, layout="fm", dtype=F32),
        _Seg(w[6], HEAD_DIM, b_qg, qk_scale, "fm"),
        _Seg(w[7], HEAD_DIM, b_kg, 1.0, "tm"),
        _Seg(w[8], layout="vt"),
    ]
    aq_t, ak, av_t, iq_t, ik, iw_t, bq_t, bk, bv_t = _project(h, mix_g, segs)
    bias = _t5_bias_tiles(t5_bias)
    top_k = min(TOPK_MAX, s // 4)
    out_a = _dsa_attention(iq_t, iw_t, ik, aq_t, ak, av_t, bias[:A_HEADS], top_k)
    lambda_init = 0.8 - 0.6 * math.exp(-0.3 * layer_idx)
    lam_rows = jnp.stack([lq1, lk1, lq2, lk2]).astype(F32)
    out_b = _diff_attention(bq_t, bk, bv_t, bias[A_HEADS:], lam_rows, b_subln, lambda_init)
    na = A_HEADS * HEAD_DIM
    return _outproj(h, [out_a, out_b], [w_out[:na], w_out[na:]])


def _odd_mixer(h, mix_g, w_in, c_qg, c_kg, rel_bias, w_out):
    f = C_HEADS * HEAD_DIM
    segs = [
        _Seg(w_in[:, :f], HEAD_DIM, c_qg, HEAD_DIM ** -0.5, "fm"),
        _Seg(w_in[:, f:2 * f], HEAD_DIM, c_kg, 1.0, "tm"),
        _Seg(w_in[:, 2 * f:], layout="vt", tile=BAND_T),
    ]
    q_t, k, v_t = _project(h, mix_g, segs)
    out = _band_attention(q_t, k, v_t, _band_bias_tiles(rel_bias))
    return _outproj(h, [out], [w_out])


def _memory_xattn(h, mem, mg, sg, wq, wkv, qg, kg, wo):
    f = M_HEADS * M_DIM
    n_mem = mem.shape[0]
    (q_t,) = _project(h, mg, [_Seg(wq, M_DIM, qg, M_DIM ** -0.5, "fm")])
    k, v_t = _project(mem, sg, [_Seg(wkv[:, :f], M_DIM, kg, 1.0, "tm"),
                                _Seg(wkv[:, f:], layout="vt", tile=n_mem)])
    return _outproj(h, [_mem_attention(q_t, k, v_t)], [wo])


def kernel(x, mem, t5_bias,
           l0_ffn1_norm, l0_ffn1_wg, l0_ffn1_wu, l0_ffn1_wd,
           l0_mix_norm, l0_w_in, l0_a_q_norm, l0_a_k_norm, l0_idx_k_norm,
           l0_b_q_norm, l0_b_k_norm, l0_b_lq1, l0_b_lk1, l0_b_lq2, l0_b_lk2, l0_b_subln, l0_w_out,
           l0_mem_norm, l0_mem_src_norm, l0_mem_wq, l0_mem_wkv, l0_mem_q_norm, l0_mem_k_norm, l0_mem_wo,
           l0_ffn2_norm, l0_ffn2_wg, l0_ffn2_wu, l0_ffn2_wd,
           l1_ffn1_norm, l1_ffn1_wg, l1_ffn1_wu, l1_ffn1_wd,
           l1_mix_norm, l1_w_in, l1_c_q_norm, l1_c_k_norm, l1_c_rel_bias, l1_w_out,
           l1_mem_norm, l1_mem_src_norm, l1_mem_wq, l1_mem_wkv, l1_mem_q_norm, l1_mem_k_norm, l1_mem_wo,
           l1_ffn2_norm, l1_ffn2_wg, l1_ffn2_wu, l1_ffn2_wd):
    bsz, seq, d = x.shape
    assert bsz == 1 and mem.shape[0] == 1
    h = x.reshape(seq, d)
    m = mem.reshape(mem.shape[1], d)

    h = _ffn(h, l0_ffn1_norm, l0_ffn1_wg, l0_ffn1_wu, l0_ffn1_wd)
    h = _even_mixer(h, l0_mix_norm, l0_w_in, l0_a_q_norm, l0_a_k_norm, l0_idx_k_norm,
                    l0_b_q_norm, l0_b_k_norm, l0_b_lq1, l0_b_lk1, l0_b_lq2, l0_b_lk2,
                    l0_b_subln, l0_w_out, t5_bias, 0)
    h = _memory_xattn(h, m, l0_mem_norm, l0_mem_src_norm, l0_mem_wq, l0_mem_wkv,
                      l0_mem_q_norm, l0_mem_k_norm, l0_mem_wo)
    h = _ffn(h, l0_ffn2_norm, l0_ffn2_wg, l0_ffn2_wu, l0_ffn2_wd)

    h = _ffn(h, l1_ffn1_norm, l1_ffn1_wg, l1_ffn1_wu, l1_ffn1_wd)
    h = _odd_mixer(h, l1_mix_norm, l1_w_in, l1_c_q_norm, l1_c_k_norm, l1_c_rel_bias, l1_w_out)
    h = _memory_xattn(h, m, l1_mem_norm, l1_mem_src_norm, l1_mem_wq, l1_mem_wkv,
                      l1_mem_q_norm, l1_mem_k_norm, l1_mem_wo)
    h = _ffn(h, l1_ffn2_norm, l1_ffn2_wg, l1_ffn2_wu, l1_ffn2_wd)
    return h.reshape(bsz, seq, d)
```

```python
import functools
import math

import jax
import jax.numpy as jnp
import numpy as np
from jax import lax
from jax.experimental import pallas as pl
from jax.experimental.pallas import tpu as pltpu

D_MODEL = 1024
CHUNK = 64
HEAD_DIM = 64
EPS = 1e-6
A_HEADS = 8
IDX_HEADS = 8
IDX_DIM = 64
TOPK_MAX = 256
B_VDIM = 128
B_HEADS = 4
C_HEADS = 16
C_BAND = 9
REL_CLIP = 256
T5_BUCKETS = 32
T5_MAX_DIST = 128
M_HEADS = 4
M_DIM = 128
D_FF = 2816

F32 = jnp.float32
BF16 = jnp.bfloat16
I32 = jnp.int32

NEG = -1e30
INT_MIN = -(2 ** 31)
VMEM_LIMIT = 56 * 1024 * 1024

ROW_TILE = 512
FF_CHUNK = 256
TQ = 256
TK = 256
KC = 4
BAND_T = 512


def _cparams(n_axes):
    return pltpu.CompilerParams(dimension_semantics=("arbitrary",) * n_axes,
                                vmem_limit_bytes=VMEM_LIMIT)


def _dot(a, b):
    return jnp.dot(a, b, preferred_element_type=F32)


def _rms_rows(x, g):
    ms = jnp.mean(x * x, axis=-1, keepdims=True)
    return x * lax.rsqrt(ms + EPS) * g


def _ffn_body(h_ref, g_ref, wg_ref, wu_ref, wd_ref, o_ref, xn_ref, acc_ref):
    x = h_ref[...]
    xn_ref[...] = _rms_rows(x, g_ref[...]).astype(BF16)
    acc_ref[...] = jnp.zeros_like(acc_ref)

    def chunk(c, carry):
        xn = xn_ref[...]
        gate = _dot(xn, wg_ref[c])
        up = _dot(xn, wu_ref[c])
        act = (gate * jax.nn.sigmoid(gate) * up).astype(BF16)
        acc_ref[...] += _dot(act, wd_ref[c])
        return carry

    lax.fori_loop(0, wg_ref.shape[0], chunk, 0)
    o_ref[...] = x + 0.5 * acc_ref[...]


def _ffn(h, g, wg, wu, wd):
    s, d = h.shape
    ff = wg.shape[1]
    nc = ff // FF_CHUNK
    wg3 = wg.astype(BF16).reshape(d, nc, FF_CHUNK).transpose(1, 0, 2)
    wu3 = wu.astype(BF16).reshape(d, nc, FF_CHUNK).transpose(1, 0, 2)
    wd3 = wd.astype(BF16).reshape(nc, FF_CHUNK, d)
    tm = min(ROW_TILE, s)
    const3 = lambda i: (0, 0, 0)
    return pl.pallas_call(
        _ffn_body,
        grid=(s // tm,),
        in_specs=[
            pl.BlockSpec((tm, d), lambda i: (i, 0)),
            pl.BlockSpec((1, d), lambda i: (0, 0)),
            pl.BlockSpec((nc, d, FF_CHUNK), const3, pipeline_mode=pl.Buffered(1)),
            pl.BlockSpec((nc, d, FF_CHUNK), const3, pipeline_mode=pl.Buffered(1)),
            pl.BlockSpec((nc, FF_CHUNK, d), const3, pipeline_mode=pl.Buffered(1)),
        ],
        out_specs=pl.BlockSpec((tm, d), lambda i: (i, 0)),
        out_shape=jax.ShapeDtypeStruct((s, d), F32),
        scratch_shapes=[pltpu.VMEM((tm, d), BF16), pltpu.VMEM((tm, d), F32)],
        compiler_params=_cparams(1),
        name="ffn",
    )(h, g.reshape(1, d), wg3, wu3, wd3)


class _Seg:
    def __init__(self, w, head_dim=None, gain=None, scale=1.0, layout="fm",
                 dtype=BF16, tile=TK, pad=0):
        self.w = w
        self.head_dim = head_dim
        self.gain = gain
        self.scale = scale
        self.layout = layout
        self.dtype = dtype
        self.tile = tile
        self.pad = pad


def _proj_body(segs, x_ref, g_ref, *refs):
    n = len(segs)
    w_refs = refs[:n]
    gains = [r for r in refs[n:2 * n]]
    outs = refs[2 * n:]
    xn = _rms_rows(x_ref[...], g_ref[...]).astype(BF16)
    tm = xn.shape[0]
    for seg, w_ref, gain_ref, o_ref in zip(segs, w_refs, gains, outs):
        yt = lax.dot_general(w_ref[...], xn, (((1,), (1,)), ((), ())),
                             preferred_element_type=F32)
        f = yt.shape[0]
        if seg.head_dim is not None:
            hd = seg.head_dim
            y3 = yt.reshape(f // hd, hd, tm)
            ms = jnp.mean(y3 * y3, axis=1, keepdims=True)
            y3 = y3 * lax.rsqrt(ms + EPS) * gain_ref[...][None]
            yt = y3.reshape(f, tm)
        if seg.scale != 1.0:
            yt = yt * seg.scale
        if seg.pad:
            yt = jnp.concatenate([yt, jnp.zeros((seg.pad, tm), F32)], axis=0)
        if seg.layout == "fm":
            o_ref[...] = yt.astype(seg.dtype)
        elif seg.layout == "tm":
            o_ref[...] = yt.T.astype(seg.dtype)
        else:
            for t in range(tm // seg.tile):
                o_ref[t] = yt[:, t * seg.tile:(t + 1) * seg.tile].astype(seg.dtype)


def _project(x, g, segs):
    s, d = x.shape
    tm = min(ROW_TILE, s)
    in_specs = [pl.BlockSpec((tm, d), lambda i: (i, 0)),
                pl.BlockSpec((1, d), lambda i: (0, 0))]
    args = [x, g.reshape(1, d)]
    for seg in segs:
        wt = seg.w.T.astype(BF16)
        args.append(wt)
        in_specs.append(pl.BlockSpec(wt.shape, lambda i: (0, 0)))
    for seg in segs:
        hd = seg.head_dim or 8
        gain = seg.gain if seg.gain is not None else jnp.ones((hd,), F32)
        args.append(gain.astype(F32).reshape(hd, 1))
        in_specs.append(pl.BlockSpec((hd, 1), lambda i: (0, 0)))
    out_shapes, out_specs = [], []
    for seg in segs:
        f = seg.w.shape[1] + seg.pad
        if seg.layout == "fm":
            out_shapes.append(jax.ShapeDtypeStruct((f, s), seg.dtype))
            out_specs.append(pl.BlockSpec((f, tm), lambda i: (0, i)))
        elif seg.layout == "tm":
            out_shapes.append(jax.ShapeDtypeStruct((s, f), seg.dtype))
            out_specs.append(pl.BlockSpec((tm, f), lambda i: (i, 0)))
        else:
            nt = tm // seg.tile
            out_shapes.append(jax.ShapeDtypeStruct((s // seg.tile, f, seg.tile), seg.dtype))
            out_specs.append(pl.BlockSpec((nt, f, seg.tile), lambda i: (i, 0, 0)))
    return pl.pallas_call(
        functools.partial(_proj_body, segs),
        grid=(s // tm,),
        in_specs=in_specs,
        out_specs=out_specs,
        out_shape=out_shapes,
        compiler_params=_cparams(1),
        name="project",
    )(*args)


def _outproj_body(n, h_ref, *refs):
    acc = h_ref[...]
    for ot_ref, w_ref in zip(refs[:n], refs[n:2 * n]):
        acc = acc + _dot(ot_ref[...].T.astype(BF16), w_ref[...])
    refs[2 * n][...] = acc


def _outproj(h, ots, ws):
    s, d = h.shape
    tm = min(ROW_TILE, s)
    n = len(ots)
    in_specs = [pl.BlockSpec((tm, d), lambda i: (i, 0))]
    in_specs += [pl.BlockSpec((ot.shape[0], tm), lambda i: (0, i)) for ot in ots]
    in_specs += [pl.BlockSpec(w.shape, lambda i: (0, 0)) for w in ws]
    return pl.pallas_call(
        functools.partial(_outproj_body, n),
        grid=(s // tm,),
        in_specs=in_specs,
        out_specs=pl.BlockSpec((tm, d), lambda i: (i, 0)),
        out_shape=jax.ShapeDtypeStruct((s, d), F32),
        compiler_params=_cparams(1),
        name="outproj",
    )(h, *ots, *[w.astype(BF16) for w in ws])


def _softmax_step(s, vts, m_ref, l_ref, acc_ref, idx):
    m_old = m_ref[idx]
    m_new = jnp.maximum(m_old, jnp.max(s, axis=0, keepdims=True))
    alpha = jnp.exp(m_old - m_new)
    p = jnp.exp(s - m_new)
    l_ref[idx] = alpha * l_ref[idx] + jnp.sum(p, axis=0, keepdims=True)
    pb = p.astype(BF16)
    n = s.shape[0] // len(vts)
    pv = _dot(vts[0], pb[:n])
    for u in range(1, len(vts)):
        pv = pv + _dot(vts[u], pb[u * n:(u + 1) * n])
    acc_ref[idx] = alpha * acc_ref[idx] + pv
    m_ref[idx] = m_new


def _chunk_scores(k_ref, q_half, bias_ref, head, i, c):
    parts = []
    for u in range(KC):
        j = c * KC + u
        k0 = pl.multiple_of(j * TK, TK)
        dist = i - j
        which = jnp.where(dist < 0, 3, jnp.minimum(dist, 2))
        parts.append(_dot(k_ref[pl.ds(k0, TK), :], q_half) + bias_ref[head, which])
    return jnp.concatenate(parts, axis=0)


def _init_state(m_ref, l_ref, acc_ref):
    m_ref[...] = jnp.full(m_ref.shape, NEG, F32)
    l_ref[...] = jnp.zeros_like(l_ref)
    acc_ref[...] = jnp.zeros_like(acc_ref)


def _half_masked(qpair):
    row = lax.broadcasted_iota(I32, qpair.shape, 0)
    zero = jnp.zeros_like(qpair)
    return jnp.where(row < HEAD_DIM, qpair, zero), jnp.where(row >= HEAD_DIM, qpair, zero)


def _t5_bucket(rel):
    nb = T5_BUCKETS // 2
    max_exact = nb // 2
    offset = (rel < 0).astype(jnp.int32) * nb
    n = jnp.abs(rel)
    nf = jnp.maximum(n, 1).astype(jnp.float32)
    large = max_exact + (jnp.log(nf / max_exact) / math.log(T5_MAX_DIST / max_exact)
                         * (nb - max_exact)).astype(jnp.int32)
    large = jnp.minimum(large, nb - 1)
    return offset + jnp.where(n < max_exact, n, large)


def _toeplitz(u, n_keys, n_q):
    hn, length = u.shape
    r = jnp.tile(u, (1, n_keys + 1))[:, :n_keys * (length + 1)].reshape(hn, n_keys, length + 1)
    return r[:, ::-1, :n_q]


def _t5_bias_tiles(table):
    r = jnp.arange(-(TK - 1), TQ, dtype=jnp.int32)
    sk = jnp.arange(TK, dtype=jnp.int32)[:, None]
    tq = jnp.arange(TQ, dtype=jnp.int32)[None, :]
    tiles = []
    for dist in range(3):
        u = table.astype(F32)[_t5_bucket(dist * TK + r)].T
        b = _toeplitz(u, TK, TQ)
        if dist == 0:
            b = jnp.where(((sk // CHUNK) <= (tq // CHUNK))[None], b, NEG)
        tiles.append(b)
    tiles.append(jnp.full_like(tiles[0], NEG))
    return jnp.stack(tiles, axis=1)


def _sortable_key(x):
    b = lax.bitcast_convert_type(x, I32)
    return jnp.where(b < 0, INT_MIN - b, b)


def _dsa_body(top_k, iq_ref, iw_ref, ik_ref, q_ref, k_ref, vt_ref, bias_ref, o_ref,
              keys_ref, thr_ref, m_ref, l_ref, acc_ref):
    i = pl.program_id(0)
    g = pl.program_id(1)
    n_tiles = i + 1
    n_chunks = (i + KC) // KC

    @pl.when(g == 0)
    def _select():
        w_all = iw_ref[...]
        zpad = jnp.zeros((IDX_DIM, TQ), BF16)
        qz = [jnp.concatenate([iq_ref[h * IDX_DIM:(h + 1) * IDX_DIM, :], zpad], axis=0)
              for h in range(IDX_HEADS)]

        def score_tile(j, carry):
            k0 = pl.multiple_of(j * TK, TK)
            ikt = ik_ref[pl.ds(k0, TK), :]
            sc = jnp.zeros((TK, TQ), F32)
            for h in range(IDX_HEADS):
                sc = sc + w_all[h:h + 1, :] * jnp.maximum(_dot(ikt, qz[h]), 0.0)
            keys_ref[pl.ds(k0, TK), :] = _sortable_key(sc)
            return carry

        lax.fori_loop(0, n_tiles, score_tile, 0)

        d0 = pl.multiple_of(i * TK, TK)
        sk = lax.broadcasted_iota(I32, (TK, TQ), 0) // CHUNK
        tq = lax.broadcasted_iota(I32, (TK, TQ), 1) // CHUNK
        keys_ref[pl.ds(d0, TK), :] = jnp.where(sk <= tq, keys_ref[pl.ds(d0, TK), :], INT_MIN)

        def fill_tile(j, carry):
            keys_ref[pl.ds(pl.multiple_of(j * TK, TK), TK), :] = jnp.full((TK, TQ), INT_MIN, I32)
            return carry

        lax.fori_loop(n_tiles, n_chunks * KC, fill_tile, 0)

        def count_ge(cand):
            def body(j, acc):
                k0 = pl.multiple_of(j * TK, TK)
                ge = (keys_ref[pl.ds(k0, TK), :] >= cand).astype(I32)
                return acc + jnp.sum(ge.reshape(TK // 8, 8, TQ), axis=0)
            acc = lax.fori_loop(0, n_tiles, body, jnp.zeros((8, TQ), I32))
            return jnp.sum(acc, axis=0, keepdims=True)

        thr = jnp.where(count_ge(jnp.zeros((1, TQ), I32)) >= top_k, 0, INT_MIN).astype(I32)

        def bit_step(b, thr):
            cand = thr + lax.shift_left(jnp.int32(1), 30 - b)
            return jnp.where(count_ge(cand) >= top_k, cand, thr)

        thr = lax.fori_loop(0, 31, bit_step, thr)
        thr_ref[...] = jnp.maximum(thr, INT_MIN + 1)

    _init_state(m_ref, l_ref, acc_ref)
    q_halves = _half_masked(q_ref[...])
    thr = thr_ref[...]

    def attn_chunk(c, carry):
        c0 = pl.multiple_of(c * (KC * TK), KC * TK)
        sel = keys_ref[pl.ds(c0, KC * TK), :] >= thr
        for a in range(2):
            s = jnp.where(sel, _chunk_scores(k_ref, q_halves[a], bias_ref, a, i, c), NEG)
            vts = [vt_ref[c * KC + u, a * HEAD_DIM:(a + 1) * HEAD_DIM, :] for u in range(KC)]
            _softmax_step(s, vts, m_ref, l_ref, acc_ref, a)
        return carry

    lax.fori_loop(0, n_chunks, attn_chunk, 0)
    for a in range(2):
        o_ref[a * HEAD_DIM:(a + 1) * HEAD_DIM, :] = acc_ref[a] / l_ref[a]


def _dsa_attention(iq_t, iw_t, ik, q_t, k, v_t, bias, top_k):
    s = k.shape[0]
    n_pairs = A_HEADS // 2
    return pl.pallas_call(
        functools.partial(_dsa_body, top_k),
        grid=(s // TQ, n_pairs),
        in_specs=[
            pl.BlockSpec((IDX_HEADS * IDX_DIM, TQ), lambda i, g: (0, i)),
            pl.BlockSpec((IDX_HEADS, TQ), lambda i, g: (0, i)),
            pl.BlockSpec((s, 128), lambda i, g: (0, 0)),
            pl.BlockSpec((128, TQ), lambda i, g: (g, i)),
            pl.BlockSpec((s, 128), lambda i, g: (0, g)),
            pl.BlockSpec((s // TK, 128, TK), lambda i, g: (0, g, 0)),
            pl.BlockSpec((2, 4, TK, TQ), lambda i, g: (g, 0, 0, 0)),
        ],
        out_specs=pl.BlockSpec((128, TQ), lambda i, g: (g, i)),
        out_shape=jax.ShapeDtypeStruct((A_HEADS * HEAD_DIM, s), F32),
        scratch_shapes=[
            pltpu.VMEM((s, TQ), I32),
            pltpu.VMEM((1, TQ), I32),
            pltpu.VMEM((2, 1, TQ), F32),
            pltpu.VMEM((2, 1, TQ), F32),
            pltpu.VMEM((2, HEAD_DIM, TQ), F32),
        ],
        compiler_params=_cparams(2),
        name="dsa_attention",
    )(iq_t, iw_t, ik, q_t, k, v_t, bias)


def _diff_body(lambda_init, q_ref, k_ref, vt_ref, bias_ref, lam_ref, subln_ref, o_ref,
               m_ref, l_ref, acc_ref):
    i = pl.program_id(0)
    _init_state(m_ref, l_ref, acc_ref)
    q_halves = _half_masked(q_ref[...])

    def attn_chunk(c, carry):
        vts = [vt_ref[c * KC + u] for u in range(KC)]
        for a in range(2):
            s = _chunk_scores(k_ref, q_halves[a], bias_ref, 0, i, c)
            _softmax_step(s, vts, m_ref, l_ref, acc_ref, a)
        return carry

    lax.fori_loop(0, (i + KC) // KC, attn_chunk, 0)

    lq1, lk1, lq2, lk2 = (lam_ref[r:r + 1, :] for r in range(4))
    lam = (jnp.exp(jnp.sum(lq1 * lk1, keepdims=True)) - jnp.exp(jnp.sum(lq2 * lk2, keepdims=True))
           + lambda_init)
    o = acc_ref[0] / l_ref[0] - lam * (acc_ref[1] / l_ref[1])
    ms = jnp.mean(o * o, axis=0, keepdims=True)
    o_ref[...] = o * lax.rsqrt(ms + EPS) * subln_ref[...] * (1.0 - lambda_init)


def _diff_attention(q_t, k, v_t, bias, lam_rows, subln, lambda_init):
    s = k.shape[0]
    return pl.pallas_call(
        functools.partial(_diff_body, lambda_init),
        grid=(s // TQ, B_HEADS),
        in_specs=[
            pl.BlockSpec((128, TQ), lambda i, h: (h, i)),
            pl.BlockSpec((s, 128), lambda i, h: (0, h)),
            pl.BlockSpec((s // TK, B_VDIM, TK), lambda i, h: (0, h, 0)),
            pl.BlockSpec((1, 4, TK, TQ), lambda i, h: (h, 0, 0, 0)),
            pl.BlockSpec((4, HEAD_DIM), lambda i, h: (0, 0)),
            pl.BlockSpec((B_VDIM, 1), lambda i, h: (0, 0)),
        ],
        out_specs=pl.BlockSpec((B_VDIM, TQ), lambda i, h: (h, i)),
        out_shape=jax.ShapeDtypeStruct((B_HEADS * B_VDIM, s), F32),
        scratch_shapes=[
            pltpu.VMEM((2, 1, TQ), F32),
            pltpu.VMEM((2, 1, TQ), F32),
            pltpu.VMEM((2, B_VDIM, TQ), F32),
        ],
        compiler_params=_cparams(2),
        name="diff_attention",
    )(q_t, k, v_t, bias, lam_rows, subln.astype(F32).reshape(B_VDIM, 1))


def _band_bias_tiles(rel_bias):
    sk = jnp.arange(BAND_T, dtype=jnp.int32)[:, None]
    tq = jnp.arange(BAND_T, dtype=jnp.int32)[None, :]
    table = rel_bias.astype(F32)
    r = jnp.arange(-(BAND_T - 1), BAND_T, dtype=jnp.int32)
    tiles = []
    for blk in range(2):
        rel = (1 - blk) * BAND_T + r
        u = table[jnp.clip(rel, -REL_CLIP, REL_CLIP) + REL_CLIP].T
        kc = sk // CHUNK + blk * (BAND_T // CHUNK)
        qc = tq // CHUNK + BAND_T // CHUNK
        ok = (kc <= qc) & (kc >= qc - (C_BAND - 1))
        tiles.append(jnp.where(ok[None], _toeplitz(u, BAND_T, BAND_T), NEG))
    return jnp.stack(tiles, axis=1)


def _band_body(q_ref, kp_ref, kc_ref, vp_ref, vc_ref, bias_ref, o_ref, m_ref, l_ref, acc_ref):
    i = pl.program_id(1)
    _init_state(m_ref, l_ref, acc_ref)
    q_halves = _half_masked(q_ref[...])

    @pl.when(i > 0)
    def _prev():
        kt = kp_ref[...]
        vt = vp_ref[0]
        for a in range(2):
            s = _dot(kt, q_halves[a]) + bias_ref[a, 0]
            _softmax_step(s, [vt[a * HEAD_DIM:(a + 1) * HEAD_DIM, :]], m_ref, l_ref, acc_ref, a)

    kt = kc_ref[...]
    vt = vc_ref[0]
    for a in range(2):
        s = _dot(kt, q_halves[a]) + bias_ref[a, 1]
        _softmax_step(s, [vt[a * HEAD_DIM:(a + 1) * HEAD_DIM, :]], m_ref, l_ref, acc_ref, a)
        o_ref[a * HEAD_DIM:(a + 1) * HEAD_DIM, :] = acc_ref[a] / l_ref[a]


def _band_attention(q_t, k, v_t, bias):
    s = k.shape[0]
    t = BAND_T
    prev = lambda i: jnp.maximum(i - 1, 0)
    return pl.pallas_call(
        _band_body,
        grid=(C_HEADS // 2, s // t),
        in_specs=[
            pl.BlockSpec((128, t), lambda p, i: (p, i)),
            pl.BlockSpec((t, 128), lambda p, i: (prev(i), p)),
            pl.BlockSpec((t, 128), lambda p, i: (i, p)),
            pl.BlockSpec((1, 128, t), lambda p, i: (prev(i), p, 0)),
            pl.BlockSpec((1, 128, t), lambda p, i: (i, p, 0)),
            pl.BlockSpec((2, 2, t, t), lambda p, i: (p, 0, 0, 0)),
        ],
        out_specs=pl.BlockSpec((128, t), lambda p, i: (p, i)),
        out_shape=jax.ShapeDtypeStruct((C_HEADS * HEAD_DIM, s), F32),
        scratch_shapes=[
            pltpu.VMEM((2, 1, t), F32),
            pltpu.VMEM((2, 1, t), F32),
            pltpu.VMEM((2, HEAD_DIM, t), F32),
        ],
        compiler_params=_cparams(2),
        name="band_attention",
    )(q_t, k, k, v_t, v_t, bias)


def _mem_body(q_ref, k_ref, vt_ref, o_ref):
    for h in range(M_HEADS):
        rows = slice(h * M_DIM, (h + 1) * M_DIM)
        s = _dot(k_ref[:, rows], q_ref[rows, :])
        p = jnp.exp(s - jnp.max(s, axis=0, keepdims=True))
        denom = jnp.sum(p, axis=0, keepdims=True)
        o_ref[rows, :] = _dot(vt_ref[0, rows, :], p.astype(BF16)) / denom


def _mem_attention(q_t, k, v_t):
    f, s = q_t.shape
    tq = min(ROW_TILE, s)
    return pl.pallas_call(
        _mem_body,
        grid=(s // tq,),
        in_specs=[
            pl.BlockSpec((f, tq), lambda i: (0, i)),
            pl.BlockSpec(k.shape, lambda i: (0, 0)),
            pl.BlockSpec(v_t.shape, lambda i: (0, 0, 0)),
        ],
        out_specs=pl.BlockSpec((f, tq), lambda i: (0, i)),
        out_shape=jax.ShapeDtypeStruct((f, s), F32),
        compiler_params=_cparams(1),
        name="mem_attention",
    )(q_t, k, v_t)


def _even_mixer(h, mix_g, w_in, a_qg, a_kg, idx_kg, b_qg, b_kg, lq1, lk1, lq2, lk2, b_subln,
                w_out, t5_bias, layer_idx):
    s = h.shape[0]
    sizes = [A_HEADS * HEAD_DIM] * 3 + [IDX_HEADS * IDX_DIM, IDX_DIM, IDX_HEADS] + \
            [B_HEADS * 2 * HEAD_DIM] * 2 + [B_HEADS * B_VDIM]
    cuts = np.cumsum([0] + sizes)
    w = [w_in[:, cuts[n]:cuts[n + 1]] for n in range(len(sizes))]
    qk_scale = HEAD_DIM ** -0.5
    segs = [
        _Seg(w[0], HEAD_DIM, a_qg, qk_scale, "fm"),
        _Seg(w[1], HEAD_DIM, a_kg, 1.0, "tm"),
        _Seg(w[2], layout="vt"),
        _Seg(w[3], scale=IDX_DIM ** -0.5, layout="fm"),
        _Seg(w[4], IDX_DIM, idx_kg, 1.0, "tm", pad=IDX_DIM),
        _Seg(w[5], scale=IDX_HEADS ** -0.5, layout="fm", dtype=F32),
        _Seg(w[6], HEAD_DIM, b_qg, qk_scale, "fm"),
        _Seg(w[7], HEAD_DIM, b_kg, 1.0, "tm"),
        _Seg(w[8], layout="vt"),
    ]
    aq_t, ak, av_t, iq_t, ik, iw_t, bq_t, bk, bv_t = _project(h, mix_g, segs)
    bias = _t5_bias_tiles(t5_bias)
    assert s % (KC * TK) == 0
    top_k = min(TOPK_MAX, s // 4)
    out_a = _dsa_attention(iq_t, iw_t, ik, aq_t, ak, av_t, bias[:A_HEADS], top_k)
    lambda_init = 0.8 - 0.6 * math.exp(-0.3 * layer_idx)
    lam_rows = jnp.stack([lq1, lk1, lq2, lk2]).astype(F32)
    out_b = _diff_attention(bq_t, bk, bv_t, bias[A_HEADS:], lam_rows, b_subln, lambda_init)
    na = A_HEADS * HEAD_DIM
    return _outproj(h, [out_a, out_b], [w_out[:na], w_out[na:]])


def _odd_mixer(h, mix_g, w_in, c_qg, c_kg, rel_bias, w_out):
    f = C_HEADS * HEAD_DIM
    segs = [
        _Seg(w_in[:, :f], HEAD_DIM, c_qg, HEAD_DIM ** -0.5, "fm"),
        _Seg(w_in[:, f:2 * f], HEAD_DIM, c_kg, 1.0, "tm"),
        _Seg(w_in[:, 2 * f:], layout="vt", tile=BAND_T),
    ]
    q_t, k, v_t = _project(h, mix_g, segs)
    out = _band_attention(q_t, k, v_t, _band_bias_tiles(rel_bias))
    return _outproj(h, [out], [w_out])


def _memory_xattn(h, mem, mg, sg, wq, wkv, qg, kg, wo):
    f = M_HEADS * M_DIM
    n_mem = mem.shape[0]
    (q_t,) = _project(h, mg, [_Seg(wq, M_DIM, qg, M_DIM ** -0.5, "fm")])
    k, v_t = _project(mem, sg, [_Seg(wkv[:, :f], M_DIM, kg, 1.0, "tm"),
                                _Seg(wkv[:, f:], layout="vt", tile=n_mem)])
    return _outproj(h, [_mem_attention(q_t, k, v_t)], [wo])


def kernel(x, mem, t5_bias,
           l0_ffn1_norm, l0_ffn1_wg, l0_ffn1_wu, l0_ffn1_wd,
           l0_mix_norm, l0_w_in, l0_a_q_norm, l0_a_k_norm, l0_idx_k_norm,
           l0_b_q_norm, l0_b_k_norm, l0_b_lq1, l0_b_lk1, l0_b_lq2, l0_b_lk2, l0_b_subln, l0_w_out,
           l0_mem_norm, l0_mem_src_norm, l0_mem_wq, l0_mem_wkv, l0_mem_q_norm, l0_mem_k_norm, l0_mem_wo,
           l0_ffn2_norm, l0_ffn2_wg, l0_ffn2_wu, l0_ffn2_wd,
           l1_ffn1_norm, l1_ffn1_wg, l1_ffn1_wu, l1_ffn1_wd,
           l1_mix_norm, l1_w_in, l1_c_q_norm, l1_c_k_norm, l1_c_rel_bias, l1_w_out,
           l1_mem_norm, l1_mem_src_norm, l1_mem_wq, l1_mem_wkv, l1_mem_q_norm, l1_mem_k_norm, l1_mem_wo,
           l1_ffn2_norm, l1_ffn2_wg, l1_ffn2_wu, l1_ffn2_wd):
    bsz, seq, d = x.shape
    assert bsz == 1 and mem.shape[0] == 1
    h = x.reshape(seq, d)
    m = mem.reshape(mem.shape[1], d)

    h = _ffn(h, l0_ffn1_norm, l0_ffn1_wg, l0_ffn1_wu, l0_ffn1_wd)
    h = _even_mixer(h, l0_mix_norm, l0_w_in, l0_a_q_norm, l0_a_k_norm, l0_idx_k_norm,
                    l0_b_q_norm, l0_b_k_norm, l0_b_lq1, l0_b_lk1, l0_b_lq2, l0_b_lk2,
                    l0_b_subln, l0_w_out, t5_bias, 0)
    h = _memory_xattn(h, m, l0_mem_norm, l0_mem_src_norm, l0_mem_wq, l0_mem_wkv,
                      l0_mem_q_norm, l0_mem_k_norm, l0_mem_wo)
    h = _ffn(h, l0_ffn2_norm, l0_ffn2_wg, l0_ffn2_wu, l0_ffn2_wd)

    h = _ffn(h, l1_ffn1_norm, l1_ffn1_wg, l1_ffn1_wu, l1_ffn1_wd)
    h = _odd_mixer(h, l1_mix_norm, l1_w_in, l1_c_q_norm, l1_c_k_norm, l1_c_rel_bias, l1_w_out)
    h = _memory_xattn(h, m, l1_mem_norm, l1_mem_src_norm, l1_mem_wq, l1_mem_wkv,
                      l1_mem_q_norm, l1_mem_k_norm, l1_mem_wo)
    h = _ffn(h, l1_ffn2_norm, l1_ffn2_wg, l1_ffn2_wu, l1_ffn2_wd)
    return h.reshape(bsz, seq, d)
```

```python
import functools
import math

import jax
import jax.numpy as jnp
import numpy as np
from jax import lax
from jax.experimental import pallas as pl
from jax.experimental.pallas import tpu as pltpu

D_MODEL = 1024
CHUNK = 64
HEAD_DIM = 64
EPS = 1e-6
A_HEADS = 8
IDX_HEADS = 8
IDX_DIM = 64
TOPK_MAX = 256
B_VDIM = 128
B_HEADS = 4
C_HEADS = 16
C_BAND = 9
REL_CLIP = 256
T5_BUCKETS = 32
T5_MAX_DIST = 128
M_HEADS = 4
M_DIM = 128
D_FF = 2816

F32 = jnp.float32
BF16 = jnp.bfloat16
I32 = jnp.int32

NEG = -1e30
LOG2E = math.log2(math.e)
INT_MIN = -(2 ** 31)
VMEM_LIMIT = 56 * 1024 * 1024

ROW_TILE = 512
FF_CHUNK = 256
TQ = 256
TK = 256
KC = 4
ROWSUM_ROWS = 16
BAND_T = 512


def _cparams(n_axes):
    return pltpu.CompilerParams(dimension_semantics=("arbitrary",) * n_axes,
                                vmem_limit_bytes=VMEM_LIMIT)


def _dot(a, b):
    return jnp.dot(a, b, preferred_element_type=F32)


def _rms_rows(x, g):
    ms = jnp.mean(x * x, axis=-1, keepdims=True)
    return x * lax.rsqrt(ms + EPS) * g


def _ffn_body(h_ref, g_ref, wg_ref, wu_ref, wd_ref, o_ref, xn_ref, acc_ref):
    x = h_ref[...]
    xn_ref[...] = _rms_rows(x, g_ref[...]).astype(BF16)
    acc_ref[...] = jnp.zeros_like(acc_ref)

    def chunk(c, carry):
        xn = xn_ref[...]
        gate = _dot(xn, wg_ref[c])
        up = _dot(xn, wu_ref[c])
        act = (gate * jax.nn.sigmoid(gate) * up).astype(BF16)
        acc_ref[...] += _dot(act, wd_ref[c])
        return carry

    lax.fori_loop(0, wg_ref.shape[0], chunk, 0)
    o_ref[...] = x + 0.5 * acc_ref[...]


def _ffn(h, g, wg, wu, wd):
    s, d = h.shape
    ff = wg.shape[1]
    nc = ff // FF_CHUNK
    wg3 = wg.astype(BF16).reshape(d, nc, FF_CHUNK).transpose(1, 0, 2)
    wu3 = wu.astype(BF16).reshape(d, nc, FF_CHUNK).transpose(1, 0, 2)
    wd3 = wd.astype(BF16).reshape(nc, FF_CHUNK, d)
    tm = min(ROW_TILE, s)
    const3 = lambda i: (0, 0, 0)
    return pl.pallas_call(
        _ffn_body,
        grid=(s // tm,),
        in_specs=[
            pl.BlockSpec((tm, d), lambda i: (i, 0)),
            pl.BlockSpec((1, d), lambda i: (0, 0)),
            pl.BlockSpec((nc, d, FF_CHUNK), const3, pipeline_mode=pl.Buffered(1)),
            pl.BlockSpec((nc, d, FF_CHUNK), const3, pipeline_mode=pl.Buffered(1)),
            pl.BlockSpec((nc, FF_CHUNK, d), const3, pipeline_mode=pl.Buffered(1)),
        ],
        out_specs=pl.BlockSpec((tm, d), lambda i: (i, 0)),
        out_shape=jax.ShapeDtypeStruct((s, d), F32),
        scratch_shapes=[pltpu.VMEM((tm, d), BF16), pltpu.VMEM((tm, d), F32)],
        compiler_params=_cparams(1),
        name="ffn",
    )(h, g.reshape(1, d), wg3, wu3, wd3)


class _Seg:
    def __init__(self, w, head_dim=None, gain=None, scale=1.0, layout="fm",
                 dtype=BF16, tile=TK, pad=0):
        self.w = w
        self.head_dim = head_dim
        self.gain = gain
        self.scale = scale
        self.layout = layout
        self.dtype = dtype
        self.tile = tile
        self.pad = pad


def _proj_body(segs, x_ref, g_ref, *refs):
    n = len(segs)
    w_refs = refs[:n]
    gains = [r for r in refs[n:2 * n]]
    outs = refs[2 * n:]
    xn = _rms_rows(x_ref[...], g_ref[...]).astype(BF16)
    tm = xn.shape[0]
    for seg, w_ref, gain_ref, o_ref in zip(segs, w_refs, gains, outs):
        yt = lax.dot_general(w_ref[...], xn, (((1,), (1,)), ((), ())),
                             preferred_element_type=F32)
        f = yt.shape[0]
        if seg.head_dim is not None:
            hd = seg.head_dim
            y3 = yt.reshape(f // hd, hd, tm)
            ms = jnp.mean(y3 * y3, axis=1, keepdims=True)
            y3 = y3 * lax.rsqrt(ms + EPS) * gain_ref[...][None]
            yt = y3.reshape(f, tm)
        if seg.scale != 1.0:
            yt = yt * seg.scale
        if seg.pad:
            yt = jnp.concatenate([yt, jnp.zeros((seg.pad, tm), F32)], axis=0)
        if seg.layout == "fm":
            o_ref[...] = yt.astype(seg.dtype)
        elif seg.layout == "tm":
            o_ref[...] = yt.T.astype(seg.dtype)
        else:
            for t in range(tm // seg.tile):
                o_ref[t] = yt[:, t * seg.tile:(t + 1) * seg.tile].astype(seg.dtype)


def _project(x, g, segs):
    s, d = x.shape
    tm = min(ROW_TILE, s)
    in_specs = [pl.BlockSpec((tm, d), lambda i: (i, 0)),
                pl.BlockSpec((1, d), lambda i: (0, 0))]
    args = [x, g.reshape(1, d)]
    for seg in segs:
        wt = seg.w.T.astype(BF16)
        args.append(wt)
        in_specs.append(pl.BlockSpec(wt.shape, lambda i: (0, 0)))
    for seg in segs:
        hd = seg.head_dim or 8
        gain = seg.gain if seg.gain is not None else jnp.ones((hd,), F32)
        args.append(gain.astype(F32).reshape(hd, 1))
        in_specs.append(pl.BlockSpec((hd, 1), lambda i: (0, 0)))
    out_shapes, out_specs = [], []
    for seg in segs:
        f = seg.w.shape[1] + seg.pad
        if seg.layout == "fm":
            out_shapes.append(jax.ShapeDtypeStruct((f, s), seg.dtype))
            out_specs.append(pl.BlockSpec((f, tm), lambda i: (0, i)))
        elif seg.layout == "tm":
            out_shapes.append(jax.ShapeDtypeStruct((s, f), seg.dtype))
            out_specs.append(pl.BlockSpec((tm, f), lambda i: (i, 0)))
        else:
            nt = tm // seg.tile
            out_shapes.append(jax.ShapeDtypeStruct((s // seg.tile, f, seg.tile), seg.dtype))
            out_specs.append(pl.BlockSpec((nt, f, seg.tile), lambda i: (i, 0, 0)))
    return pl.pallas_call(
        functools.partial(_proj_body, segs),
        grid=(s // tm,),
        in_specs=in_specs,
        out_specs=out_specs,
        out_shape=out_shapes,
        compiler_params=_cparams(1),
        name="project",
    )(*args)


def _outproj_body(n, h_ref, *refs):
    acc = h_ref[...]
    for ot_ref, w_ref in zip(refs[:n], refs[n:2 * n]):
        acc = acc + _dot(ot_ref[...].T.astype(BF16), w_ref[...])
    refs[2 * n][...] = acc


def _outproj(h, ots, ws):
    s, d = h.shape
    tm = min(ROW_TILE, s)
    n = len(ots)
    in_specs = [pl.BlockSpec((tm, d), lambda i: (i, 0))]
    in_specs += [pl.BlockSpec((ot.shape[0], tm), lambda i: (0, i)) for ot in ots]
    in_specs += [pl.BlockSpec(w.shape, lambda i: (0, 0)) for w in ws]
    return pl.pallas_call(
        functools.partial(_outproj_body, n),
        grid=(s // tm,),
        in_specs=in_specs,
        out_specs=pl.BlockSpec((tm, d), lambda i: (i, 0)),
        out_shape=jax.ShapeDtypeStruct((s, d), F32),
        compiler_params=_cparams(1),
        name="outproj",
    )(h, *ots, *[w.astype(BF16) for w in ws])


def _softmax_step(s, vts, m_ref, l_ref, acc_ref, idx):
    m_old = m_ref[idx]
    m_new = jnp.maximum(m_old, jnp.max(s, axis=0, keepdims=True))
    alpha = jnp.exp2(m_old - m_new)
    p = jnp.exp2(s - m_new)
    l_ref[idx] = alpha * l_ref[idx] + jnp.sum(p, axis=0, keepdims=True)
    pb = p.astype(BF16)
    n = s.shape[0] // len(vts)
    pv = _dot(vts[0], pb[:n])
    for u in range(1, len(vts)):
        pv = pv + _dot(vts[u], pb[u * n:(u + 1) * n])
    acc_ref[idx] = alpha * acc_ref[idx] + pv
    m_ref[idx] = m_new


def _softmax_step_ref(s_ref, stream, vts, m_ref, acc_ref):
    m_old = m_ref[stream]
    m_new = jnp.maximum(m_old, jnp.max(s_ref[stream], axis=0, keepdims=True))
    pv = None
    for u in range(KC):
        p = jnp.exp2(s_ref[stream, u * TK:(u + 1) * TK, :] - m_new)
        d = _dot(vts[u], p.astype(BF16))
        pv = d if pv is None else pv + d
    acc_ref[stream] = jnp.exp2(m_old - m_new) * acc_ref[stream] + pv
    m_ref[stream] = m_new


def _with_ones(vt):
    return jnp.concatenate([vt, jnp.ones((ROWSUM_ROWS, vt.shape[1]), vt.dtype)], axis=0)


def _normalised(acc, dv):
    return acc[:dv] / acc[dv:dv + 1]


def _score_pipeline(i, n_chunks, k_ref, q_halves, bias_ref, heads, mask_fn, consume,
                    s_even, s_odd):
    def produce(near, c, s_ref):
        for u in range(KC):
            j = c * KC + u
            k0 = pl.multiple_of(j * TK, TK)
            kt = k_ref[pl.ds(k0, TK), :]
            dist = i - j
            which = jnp.where(dist < 0, 3, jnp.minimum(dist, 2))
            sel = None if mask_fn is None else mask_fn(k0)
            for a, qh in enumerate(q_halves):
                s = _dot(kt, qh)
                if near:
                    s = s + bias_ref[heads[a], which]
                s_ref[a, u * TK:(u + 1) * TK, :] = s if sel is None else jnp.where(sel, s, NEG)

    def pair(near, p, carry):
        c = 2 * p
        produce(near, c + 1, s_odd)
        consume(c, s_even)
        produce(near, jnp.minimum(c + 2, n_chunks - 1), s_even)
        consume(c + 1, s_odd)
        return carry

    n_far = jnp.maximum(i - 1, 0) // KC
    far_pairs = jnp.maximum(n_far - 1, 0) // 2
    produce(True, 0, s_even)
    lax.fori_loop(0, far_pairs, functools.partial(pair, False), 0)
    lax.fori_loop(far_pairs, n_chunks // 2, functools.partial(pair, True), 0)

    @pl.when(n_chunks % 2 == 1)
    def _last():
        consume(n_chunks - 1, s_even)


def _init_state(m_ref, l_ref, acc_ref):
    m_ref[...] = jnp.full(m_ref.shape, NEG, F32)
    if l_ref is not None:
        l_ref[...] = jnp.zeros_like(l_ref)
    acc_ref[...] = jnp.zeros_like(acc_ref)


def _half_masked(qpair):
    row = lax.broadcasted_iota(I32, qpair.shape, 0)
    zero = jnp.zeros_like(qpair)
    return jnp.where(row < HEAD_DIM, qpair, zero), jnp.where(row >= HEAD_DIM, qpair, zero)


def _t5_bucket(rel):
    nb = T5_BUCKETS // 2
    max_exact = nb // 2
    offset = (rel < 0).astype(jnp.int32) * nb
    n = jnp.abs(rel)
    nf = jnp.maximum(n, 1).astype(jnp.float32)
    large = max_exact + (jnp.log(nf / max_exact) / math.log(T5_MAX_DIST / max_exact)
                         * (nb - max_exact)).astype(jnp.int32)
    large = jnp.minimum(large, nb - 1)
    return offset + jnp.where(n < max_exact, n, large)


def _toeplitz(u, n_keys, n_q):
    hn, length = u.shape
    w = jnp.concatenate([u[:, n_keys - 1:], u[:, :n_keys - 1]], axis=1)
    r = jnp.tile(w, (1, n_keys))[:, :n_keys * (length - 1)].reshape(hn, n_keys, length - 1)
    return r[:, :, :n_q]


def _t5_bias_tiles(table):
    table = table.astype(F32)
    far = table[_t5_bucket(jnp.full((1,), 2 * TK, jnp.int32))].T
    r = jnp.arange(-(TK - 1), TQ, dtype=jnp.int32)
    sk = jnp.arange(TK, dtype=jnp.int32)[:, None]
    tq = jnp.arange(TQ, dtype=jnp.int32)[None, :]
    tiles = []
    for dist in range(2):
        u = (table[_t5_bucket(dist * TK + r)].T - far) * LOG2E
        b = _toeplitz(u, TK, TQ)
        if dist == 0:
            b = jnp.where(((sk // CHUNK) <= (tq // CHUNK))[None], b, NEG)
        tiles.append(b)
    tiles.append(jnp.zeros_like(tiles[0]))
    tiles.append(jnp.full_like(tiles[0], NEG))
    return jnp.stack(tiles, axis=1)


def _sortable_key(x):
    b = lax.bitcast_convert_type(x, I32)
    return jnp.where(b < 0, INT_MIN - b, b)


def _dsa_body(top_k, iq_ref, iw_ref, ik_ref, q_ref, k_ref, vt_ref, bias_ref, o_ref,
              keys_ref, thr_ref, m_ref, acc_ref, s_even, s_odd):
    i = pl.program_id(0)
    g = pl.program_id(1)
    n_tiles = i + 1
    n_chunks = (i + KC) // KC

    @pl.when(g == 0)
    def _select():
        w_all = iw_ref[...]
        zpad = jnp.zeros((IDX_DIM, TQ), BF16)
        qz = [jnp.concatenate([iq_ref[h * IDX_DIM:(h + 1) * IDX_DIM, :], zpad], axis=0)
              for h in range(IDX_HEADS)]

        def score_tile(j, carry):
            k0 = pl.multiple_of(j * TK, TK)
            ikt = ik_ref[pl.ds(k0, TK), :]
            sc = jnp.zeros((TK, TQ), F32)
            for h in range(IDX_HEADS):
                sc = sc + w_all[h:h + 1, :] * jnp.maximum(_dot(ikt, qz[h]), 0.0)
            keys_ref[pl.ds(k0, TK), :] = _sortable_key(sc)
            return carry

        lax.fori_loop(0, n_tiles, score_tile, 0)

        d0 = pl.multiple_of(i * TK, TK)
        sk = lax.broadcasted_iota(I32, (TK, TQ), 0) // CHUNK
        tq = lax.broadcasted_iota(I32, (TK, TQ), 1) // CHUNK
        keys_ref[pl.ds(d0, TK), :] = jnp.where(sk <= tq, keys_ref[pl.ds(d0, TK), :], INT_MIN)

        def fill_tile(j, carry):
            keys_ref[pl.ds(pl.multiple_of(j * TK, TK), TK), :] = jnp.full((TK, TQ), INT_MIN, I32)
            return carry

        lax.fori_loop(n_tiles, n_chunks * KC, fill_tile, 0)

        def count_ge(cand):
            def body(j, acc):
                k0 = pl.multiple_of(j * TK, TK)
                ge = (keys_ref[pl.ds(k0, TK), :] >= cand).astype(I32)
                return acc + jnp.sum(ge.reshape(TK // 8, 8, TQ), axis=0)
            acc = lax.fori_loop(0, n_tiles, body, jnp.zeros((8, TQ), I32))
            return jnp.sum(acc, axis=0, keepdims=True)

        thr = jnp.where(count_ge(jnp.zeros((1, TQ), I32)) >= top_k, 0, INT_MIN).astype(I32)

        def bit_step(b, thr):
            cand = thr + lax.shift_left(jnp.int32(1), 30 - b)
            return jnp.where(count_ge(cand) >= top_k, cand, thr)

        thr = lax.fori_loop(0, 31, bit_step, thr)
        thr_ref[...] = jnp.maximum(thr, INT_MIN + 1)

    _init_state(m_ref, None, acc_ref)
    q_halves = _half_masked(q_ref[...])
    thr = thr_ref[...]

    def selected(k0):
        return keys_ref[pl.ds(k0, TK), :] >= thr

    def consume(c, s_ref):
        for a in range(2):
            vts = [_with_ones(vt_ref[c * KC + u, a * HEAD_DIM:(a + 1) * HEAD_DIM, :])
                   for u in range(KC)]
            _softmax_step_ref(s_ref, a, vts, m_ref, acc_ref)

    _score_pipeline(i, n_chunks, k_ref, q_halves, bias_ref, (0, 1), selected, consume,
                    s_even, s_odd)
    for a in range(2):
        o_ref[a * HEAD_DIM:(a + 1) * HEAD_DIM, :] = _normalised(acc_ref[a], HEAD_DIM)


def _dsa_attention(iq_t, iw_t, ik, q_t, k, v_t, bias, top_k):
    s = k.shape[0]
    n_pairs = A_HEADS // 2
    return pl.pallas_call(
        functools.partial(_dsa_body, top_k),
        grid=(s // TQ, n_pairs),
        in_specs=[
            pl.BlockSpec((IDX_HEADS * IDX_DIM, TQ), lambda i, g: (0, i)),
            pl.BlockSpec((IDX_HEADS, TQ), lambda i, g: (0, i)),
            pl.BlockSpec((s, 128), lambda i, g: (0, 0)),
            pl.BlockSpec((128, TQ), lambda i, g: (g, i)),
            pl.BlockSpec((s, 128), lambda i, g: (0, g)),
            pl.BlockSpec((s // TK, 128, TK), lambda i, g: (0, g, 0)),
            pl.BlockSpec((2, 4, TK, TQ), lambda i, g: (g, 0, 0, 0)),
        ],
        out_specs=pl.BlockSpec((128, TQ), lambda i, g: (g, i)),
        out_shape=jax.ShapeDtypeStruct((A_HEADS * HEAD_DIM, s), F32),
        scratch_shapes=[
            pltpu.VMEM((s, TQ), I32),
            pltpu.VMEM((1, TQ), I32),
            pltpu.VMEM((2, 1, TQ), F32),
            pltpu.VMEM((2, HEAD_DIM + ROWSUM_ROWS, TQ), F32),
            pltpu.VMEM((2, KC * TK, TQ), F32),
            pltpu.VMEM((2, KC * TK, TQ), F32),
        ],
        compiler_params=_cparams(2),
        name="dsa_attention",
    )(iq_t, iw_t, ik, q_t, k, v_t, bias)


def _diff_body(lambda_init, q_ref, k_ref, vt_ref, bias_ref, lam_ref, subln_ref, o_ref,
               m_ref, acc_ref, s_even, s_odd):
    i = pl.program_id(0)
    _init_state(m_ref, None, acc_ref)
    q_halves = _half_masked(q_ref[...])

    def consume(c, s_ref):
        vts = [_with_ones(vt_ref[c * KC + u]) for u in range(KC)]
        for a in range(2):
            _softmax_step_ref(s_ref, a, vts, m_ref, acc_ref)

    _score_pipeline(i, (i + KC) // KC, k_ref, q_halves, bias_ref, (0, 0), None, consume,
                    s_even, s_odd)

    lq1, lk1, lq2, lk2 = (lam_ref[r:r + 1, :] for r in range(4))
    lam = (jnp.exp(jnp.sum(lq1 * lk1, keepdims=True)) - jnp.exp(jnp.sum(lq2 * lk2, keepdims=True))
           + lambda_init)
    o = _normalised(acc_ref[0], B_VDIM) - lam * _normalised(acc_ref[1], B_VDIM)
    ms = jnp.mean(o * o, axis=0, keepdims=True)
    o_ref[...] = o * lax.rsqrt(ms + EPS) * subln_ref[...] * (1.0 - lambda_init)


def _diff_attention(q_t, k, v_t, bias, lam_rows, subln, lambda_init):
    s = k.shape[0]
    return pl.pallas_call(
        functools.partial(_diff_body, lambda_init),
        grid=(s // TQ, B_HEADS),
        in_specs=[
            pl.BlockSpec((128, TQ), lambda i, h: (h, i)),
            pl.BlockSpec((s, 128), lambda i, h: (0, h)),
            pl.BlockSpec((s // TK, B_VDIM, TK), lambda i, h: (0, h, 0)),
            pl.BlockSpec((1, 4, TK, TQ), lambda i, h: (h, 0, 0, 0)),
            pl.BlockSpec((4, HEAD_DIM), lambda i, h: (0, 0)),
            pl.BlockSpec((B_VDIM, 1), lambda i, h: (0, 0)),
        ],
        out_specs=pl.BlockSpec((B_VDIM, TQ), lambda i, h: (h, i)),
        out_shape=jax.ShapeDtypeStruct((B_HEADS * B_VDIM, s), F32),
        scratch_shapes=[
            pltpu.VMEM((2, 1, TQ), F32),
            pltpu.VMEM((2, B_VDIM + ROWSUM_ROWS, TQ), F32),
            pltpu.VMEM((2, KC * TK, TQ), F32),
            pltpu.VMEM((2, KC * TK, TQ), F32),
        ],
        compiler_params=_cparams(2),
        name="diff_attention",
    )(q_t, k, v_t, bias, lam_rows, subln.astype(F32).reshape(B_VDIM, 1))


def _band_bias_tiles(rel_bias):
    sk = jnp.arange(BAND_T, dtype=jnp.int32)[:, None]
    tq = jnp.arange(BAND_T, dtype=jnp.int32)[None, :]
    table = rel_bias.astype(F32)
    r = jnp.arange(-(BAND_T - 1), BAND_T, dtype=jnp.int32)
    tiles = []
    for blk in range(2):
        rel = (1 - blk) * BAND_T + r
        u = table[jnp.clip(rel, -REL_CLIP, REL_CLIP) + REL_CLIP].T
        kc = sk // CHUNK + blk * (BAND_T // CHUNK)
        qc = tq // CHUNK + BAND_T // CHUNK
        ok = (kc <= qc) & (kc >= qc - (C_BAND - 1))
        tiles.append(jnp.where(ok[None], _toeplitz(u * LOG2E, BAND_T, BAND_T), NEG))
    return jnp.stack(tiles, axis=1)


def _band_body(q_ref, kp_ref, kc_ref, vp_ref, vc_ref, bias_ref, o_ref, m_ref, l_ref, acc_ref):
    i = pl.program_id(1)
    _init_state(m_ref, l_ref, acc_ref)
    q_halves = _half_masked(q_ref[...])

    @pl.when(i > 0)
    def _prev():
        kt = kp_ref[...]
        vt = vp_ref[0]
        for a in range(2):
            s = _dot(kt, q_halves[a]) + bias_ref[a, 0]
            _softmax_step(s, [vt[a * HEAD_DIM:(a + 1) * HEAD_DIM, :]], m_ref, l_ref, acc_ref, a)

    kt = kc_ref[...]
    vt = vc_ref[0]
    for a in range(2):
        s = _dot(kt, q_halves[a]) + bias_ref[a, 1]
        _softmax_step(s, [vt[a * HEAD_DIM:(a + 1) * HEAD_DIM, :]], m_ref, l_ref, acc_ref, a)
        o_ref[a * HEAD_DIM:(a + 1) * HEAD_DIM, :] = acc_ref[a] / l_ref[a]


def _band_attention(q_t, k, v_t, bias):
    s = k.shape[0]
    t = BAND_T
    prev = lambda i: jnp.maximum(i - 1, 0)
    return pl.pallas_call(
        _band_body,
        grid=(C_HEADS // 2, s // t),
        in_specs=[
            pl.BlockSpec((128, t), lambda p, i: (p, i)),
            pl.BlockSpec((t, 128), lambda p, i: (prev(i), p)),
            pl.BlockSpec((t, 128), lambda p, i: (i, p)),
            pl.BlockSpec((1, 128, t), lambda p, i: (prev(i), p, 0)),
            pl.BlockSpec((1, 128, t), lambda p, i: (i, p, 0)),
            pl.BlockSpec((2, 2, t, t), lambda p, i: (p, 0, 0, 0)),
        ],
        out_specs=pl.BlockSpec((128, t), lambda p, i: (p, i)),
        out_shape=jax.ShapeDtypeStruct((C_HEADS * HEAD_DIM, s), F32),
        scratch_shapes=[
            pltpu.VMEM((2, 1, t), F32),
            pltpu.VMEM((2, 1, t), F32),
            pltpu.VMEM((2, HEAD_DIM, t), F32),
        ],
        compiler_params=_cparams(2),
        name="band_attention",
    )(q_t, k, k, v_t, v_t, bias)


def _mem_body(q_ref, k_ref, vt_ref, o_ref):
    for h in range(M_HEADS):
        rows = slice(h * M_DIM, (h + 1) * M_DIM)
        s = _dot(k_ref[:, rows], q_ref[rows, :])
        p = jnp.exp2(s - jnp.max(s, axis=0, keepdims=True))
        denom = jnp.sum(p, axis=0, keepdims=True)
        o_ref[rows, :] = _dot(vt_ref[0, rows, :], p.astype(BF16)) / denom


def _mem_attention(q_t, k, v_t):
    f, s = q_t.shape
    tq = min(ROW_TILE, s)
    return pl.pallas_call(
        _mem_body,
        grid=(s // tq,),
        in_specs=[
            pl.BlockSpec((f, tq), lambda i: (0, i)),
            pl.BlockSpec(k.shape, lambda i: (0, 0)),
            pl.BlockSpec(v_t.shape, lambda i: (0, 0, 0)),
        ],
        out_specs=pl.BlockSpec((f, tq), lambda i: (0, i)),
        out_shape=jax.ShapeDtypeStruct((f, s), F32),
        compiler_params=_cparams(1),
        name="mem_attention",
    )(q_t, k, v_t)


def _even_mixer(h, mix_g, w_in, a_qg, a_kg, idx_kg, b_qg, b_kg, lq1, lk1, lq2, lk2, b_subln,
                w_out, t5_bias, layer_idx):
    s = h.shape[0]
    sizes = [A_HEADS * HEAD_DIM] * 3 + [IDX_HEADS * IDX_DIM, IDX_DIM, IDX_HEADS] + \
            [B_HEADS * 2 * HEAD_DIM] * 2 + [B_HEADS * B_VDIM]
    cuts = np.cumsum([0] + sizes)
    w = [w_in[:, cuts[n]:cuts[n + 1]] for n in range(len(sizes))]
    qk_scale = HEAD_DIM ** -0.5 * LOG2E
    segs = [
        _Seg(w[0], HEAD_DIM, a_qg, qk_scale, "fm"),
        _Seg(w[1], HEAD_DIM, a_kg, 1.0, "tm"),
        _Seg(w[2], layout="vt"),
        _Seg(w[3], scale=IDX_DIM ** -0.5, layout="fm"),
        _Seg(w[4], IDX_DIM, idx_kg, 1.0, "tm", pad=IDX_DIM),
        _Seg(w[5], scale=IDX_HEADS ** -0.5, layout="fm", dtype=F32),
        _Seg(w[6], HEAD_DIM, b_qg, qk_scale, "fm"),
        _Seg(w[7], HEAD_DIM, b_kg, 1.0, "tm"),
        _Seg(w[8], layout="vt"),
    ]
    aq_t, ak, av_t, iq_t, ik, iw_t, bq_t, bk, bv_t = _project(h, mix_g, segs)
    bias = _t5_bias_tiles(t5_bias)
    assert s % (KC * TK) == 0
    top_k = min(TOPK_MAX, s // 4)
    out_a = _dsa_attention(iq_t, iw_t, ik, aq_t, ak, av_t, bias[:A_HEADS], top_k)
    lambda_init = 0.8 - 0.6 * math.exp(-0.3 * layer_idx)
    lam_rows = jnp.stack([lq1, lk1, lq2, lk2]).astype(F32)
    out_b = _diff_attention(bq_t, bk, bv_t, bias[A_HEADS:], lam_rows, b_subln, lambda_init)
    na = A_HEADS * HEAD_DIM
    return _outproj(h, [out_a, out_b], [w_out[:na], w_out[na:]])


def _odd_mixer(h, mix_g, w_in, c_qg, c_kg, rel_bias, w_out):
    f = C_HEADS * HEAD_DIM
    segs = [
        _Seg(w_in[:, :f], HEAD_DIM, c_qg, HEAD_DIM ** -0.5 * LOG2E, "fm"),
        _Seg(w_in[:, f:2 * f], HEAD_DIM, c_kg, 1.0, "tm"),
        _Seg(w_in[:, 2 * f:], layout="vt", tile=BAND_T),
    ]
    q_t, k, v_t = _project(h, mix_g, segs)
    out = _band_attention(q_t, k, v_t, _band_bias_tiles(rel_bias))
    return _outproj(h, [out], [w_out])


def _memory_xattn(h, mem, mg, sg, wq, wkv, qg, kg, wo):
    f = M_HEADS * M_DIM
    n_mem = mem.shape[0]
    (q_t,) = _project(h, mg, [_Seg(wq, M_DIM, qg, M_DIM ** -0.5 * LOG2E, "fm")])
    k, v_t = _project(mem, sg, [_Seg(wkv[:, :f], M_DIM, kg, 1.0, "tm"),
                                _Seg(wkv[:, f:], layout="vt", tile=n_mem)])
    return _outproj(h, [_mem_attention(q_t, k, v_t)], [wo])


def kernel(x, mem, t5_bias,
           l0_ffn1_norm, l0_ffn1_wg, l0_ffn1_wu, l0_ffn1_wd,
           l0_mix_norm, l0_w_in, l0_a_q_norm, l0_a_k_norm, l0_idx_k_norm,
           l0_b_q_norm, l0_b_k_norm, l0_b_lq1, l0_b_lk1, l0_b_lq2, l0_b_lk2, l0_b_subln, l0_w_out,
           l0_mem_norm, l0_mem_src_norm, l0_mem_wq, l0_mem_wkv, l0_mem_q_norm, l0_mem_k_norm, l0_mem_wo,
           l0_ffn2_norm, l0_ffn2_wg, l0_ffn2_wu, l0_ffn2_wd,
           l1_ffn1_norm, l1_ffn1_wg, l1_ffn1_wu, l1_ffn1_wd,
           l1_mix_norm, l1_w_in, l1_c_q_norm, l1_c_k_norm, l1_c_rel_bias, l1_w_out,
           l1_mem_norm, l1_mem_src_norm, l1_mem_wq, l1_mem_wkv, l1_mem_q_norm, l1_mem_k_norm, l1_mem_wo,
           l1_ffn2_norm, l1_ffn2_wg, l1_ffn2_wu, l1_ffn2_wd):
    bsz, seq, d = x.shape
    assert bsz == 1 and mem.shape[0] == 1
    h = x.reshape(seq, d)
    m = mem.reshape(mem.shape[1], d)

    h = _ffn(h, l0_ffn1_norm, l0_ffn1_wg, l0_ffn1_wu, l0_ffn1_wd)
    h = _even_mixer(h, l0_mix_norm, l0_w_in, l0_a_q_norm, l0_a_k_norm, l0_idx_k_norm,
                    l0_b_q_norm, l0_b_k_norm, l0_b_lq1, l0_b_lk1, l0_b_lq2, l0_b_lk2,
                    l0_b_subln, l0_w_out, t5_bias, 0)
    h = _memory_xattn(h, m, l0_mem_norm, l0_mem_src_norm, l0_mem_wq, l0_mem_wkv,
                      l0_mem_q_norm, l0_mem_k_norm, l0_mem_wo)
    h = _ffn(h, l0_ffn2_norm, l0_ffn2_wg, l0_ffn2_wu, l0_ffn2_wd)

    h = _ffn(h, l1_ffn1_norm, l1_ffn1_wg, l1_ffn1_wu, l1_ffn1_wd)
    h = _odd_mixer(h, l1_mix_norm, l1_w_in, l1_c_q_norm, l1_c_k_norm, l1_c_rel_bias, l1_w_out)
    h = _memory_xattn(h, m, l1_mem_norm, l1_mem_src_norm, l1_mem_wq, l1_mem_wkv,
                      l1_mem_q_norm, l1_mem_k_norm, l1_mem_wo)
    h = _ffn(h, l1_ffn2_norm, l1_ffn2_wg, l1_ffn2_wu, l1_ffn2_wd)
    return h.reshape(bsz, seq, d)
```

```python
import functools
import math

import jax
import jax.numpy as jnp
import numpy as np
from jax import lax
from jax.experimental import pallas as pl
from jax.experimental.pallas import tpu as pltpu

D_MODEL = 1024
CHUNK = 64
HEAD_DIM = 64
EPS = 1e-6
A_HEADS = 8
IDX_HEADS = 8
IDX_DIM = 64
TOPK_MAX = 256
B_VDIM = 128
B_HEADS = 4
C_HEADS = 16
C_BAND = 9
REL_CLIP = 256
T5_BUCKETS = 32
T5_MAX_DIST = 128
M_HEADS = 4
M_DIM = 128
D_FF = 2816

F32 = jnp.float32
BF16 = jnp.bfloat16
I32 = jnp.int32

NEG = -1e30
LOG2E = math.log2(math.e)
INT_MIN = -(2 ** 31)
MIN_NORMAL_BITS = 0x00800000
VMEM_LIMIT = 60 * 1024 * 1024

ROW_TILE = 512
FF_CHUNK = 256
TQ = 256
TK = 256
KC = 4
ROWSUM_ROWS = 16
BAND_T = 512


def _cparams(n_axes):
    return pltpu.CompilerParams(dimension_semantics=("arbitrary",) * n_axes,
                                vmem_limit_bytes=VMEM_LIMIT)


def _dot(a, b):
    return jnp.dot(a, b, preferred_element_type=F32)


def _rms_rows(x, g):
    ms = jnp.mean(x * x, axis=-1, keepdims=True)
    return x * lax.rsqrt(ms + EPS) * g


def _ffn_body(h_ref, g_ref, wg_ref, wu_ref, wd_ref, o_ref, xn_ref, acc_ref):
    x = h_ref[...]
    xn_ref[...] = _rms_rows(x, g_ref[...]).astype(BF16)
    acc_ref[...] = jnp.zeros_like(acc_ref)

    def chunk(c, carry):
        xn = xn_ref[...]
        gate = _dot(xn, wg_ref[c])
        up = _dot(xn, wu_ref[c])
        act = (gate * jax.nn.sigmoid(gate) * up).astype(BF16)
        acc_ref[...] += _dot(act, wd_ref[c])
        return carry

    lax.fori_loop(0, wg_ref.shape[0], chunk, 0)
    o_ref[...] = x + 0.5 * acc_ref[...]


def _ffn(h, g, wg, wu, wd):
    s, d = h.shape
    ff = wg.shape[1]
    nc = ff // FF_CHUNK
    wg3 = wg.astype(BF16).reshape(d, nc, FF_CHUNK).transpose(1, 0, 2)
    wu3 = wu.astype(BF16).reshape(d, nc, FF_CHUNK).transpose(1, 0, 2)
    wd3 = wd.astype(BF16).reshape(nc, FF_CHUNK, d)
    tm = min(ROW_TILE, s)
    const3 = lambda i: (0, 0, 0)
    return pl.pallas_call(
        _ffn_body,
        grid=(s // tm,),
        in_specs=[
            pl.BlockSpec((tm, d), lambda i: (i, 0)),
            pl.BlockSpec((1, d), lambda i: (0, 0)),
            pl.BlockSpec((nc, d, FF_CHUNK), const3, pipeline_mode=pl.Buffered(1)),
            pl.BlockSpec((nc, d, FF_CHUNK), const3, pipeline_mode=pl.Buffered(1)),
            pl.BlockSpec((nc, FF_CHUNK, d), const3, pipeline_mode=pl.Buffered(1)),
        ],
        out_specs=pl.BlockSpec((tm, d), lambda i: (i, 0)),
        out_shape=jax.ShapeDtypeStruct((s, d), F32),
        scratch_shapes=[pltpu.VMEM((tm, d), BF16), pltpu.VMEM((tm, d), F32)],
        compiler_params=_cparams(1),
        name="ffn",
    )(h, g.reshape(1, d), wg3, wu3, wd3)


class _Seg:
    def __init__(self, w, head_dim=None, gain=None, scale=1.0, layout="fm",
                 dtype=BF16, tile=TK, pad=0):
        self.w = w
        self.head_dim = head_dim
        self.gain = gain
        self.scale = scale
        self.layout = layout
        self.dtype = dtype
        self.tile = tile
        self.pad = pad


def _proj_body(segs, x_ref, g_ref, *refs):
    n = len(segs)
    w_refs = refs[:n]
    gains = [r for r in refs[n:2 * n]]
    outs = refs[2 * n:]
    xn = _rms_rows(x_ref[...], g_ref[...]).astype(BF16)
    tm = xn.shape[0]
    for seg, w_ref, gain_ref, o_ref in zip(segs, w_refs, gains, outs):
        yt = lax.dot_general(w_ref[...], xn, (((1,), (1,)), ((), ())),
                             preferred_element_type=F32)
        f = yt.shape[0]
        if seg.head_dim is not None:
            hd = seg.head_dim
            y3 = yt.reshape(f // hd, hd, tm)
            ms = jnp.mean(y3 * y3, axis=1, keepdims=True)
            y3 = y3 * lax.rsqrt(ms + EPS) * gain_ref[...][None]
            yt = y3.reshape(f, tm)
        if seg.scale != 1.0:
            yt = yt * seg.scale
        if seg.pad:
            yt = jnp.concatenate([yt, jnp.zeros((seg.pad, tm), F32)], axis=0)
        if seg.layout == "fm":
            o_ref[...] = yt.astype(seg.dtype)
        elif seg.layout == "tm":
            o_ref[...] = yt.T.astype(seg.dtype)
        else:
            for t in range(tm // seg.tile):
                o_ref[t] = yt[:, t * seg.tile:(t + 1) * seg.tile].astype(seg.dtype)


def _project(x, g, segs):
    s, d = x.shape
    tm = min(ROW_TILE, s)
    in_specs = [pl.BlockSpec((tm, d), lambda i: (i, 0)),
                pl.BlockSpec((1, d), lambda i: (0, 0))]
    args = [x, g.reshape(1, d)]
    for seg in segs:
        wt = seg.w.T.astype(BF16)
        args.append(wt)
        in_specs.append(pl.BlockSpec(wt.shape, lambda i: (0, 0)))
    for seg in segs:
        hd = seg.head_dim or 8
        gain = seg.gain if seg.gain is not None else jnp.ones((hd,), F32)
        args.append(gain.astype(F32).reshape(hd, 1))
        in_specs.append(pl.BlockSpec((hd, 1), lambda i: (0, 0)))
    out_shapes, out_specs = [], []
    for seg in segs:
        f = seg.w.shape[1] + seg.pad
        if seg.layout == "fm":
            out_shapes.append(jax.ShapeDtypeStruct((f, s), seg.dtype))
            out_specs.append(pl.BlockSpec((f, tm), lambda i: (0, i)))
        elif seg.layout == "tm":
            out_shapes.append(jax.ShapeDtypeStruct((s, f), seg.dtype))
            out_specs.append(pl.BlockSpec((tm, f), lambda i: (i, 0)))
        else:
            nt = tm // seg.tile
            out_shapes.append(jax.ShapeDtypeStruct((s // seg.tile, f, seg.tile), seg.dtype))
            out_specs.append(pl.BlockSpec((nt, f, seg.tile), lambda i: (i, 0, 0)))
    return pl.pallas_call(
        functools.partial(_proj_body, segs),
        grid=(s // tm,),
        in_specs=in_specs,
        out_specs=out_specs,
        out_shape=out_shapes,
        compiler_params=_cparams(1),
        name="project",
    )(*args)


def _outproj_body(n, h_ref, *refs):
    acc = h_ref[...]
    for ot_ref, w_ref in zip(refs[:n], refs[n:2 * n]):
        acc = acc + _dot(ot_ref[...].T.astype(BF16), w_ref[...])
    refs[2 * n][...] = acc


def _outproj(h, ots, ws):
    s, d = h.shape
    tm = min(ROW_TILE, s)
    n = len(ots)
    in_specs = [pl.BlockSpec((tm, d), lambda i: (i, 0))]
    in_specs += [pl.BlockSpec((ot.shape[0], tm), lambda i: (0, i)) for ot in ots]
    in_specs += [pl.BlockSpec(w.shape, lambda i: (0, 0)) for w in ws]
    return pl.pallas_call(
        functools.partial(_outproj_body, n),
        grid=(s // tm,),
        in_specs=in_specs,
        out_specs=pl.BlockSpec((tm, d), lambda i: (i, 0)),
        out_shape=jax.ShapeDtypeStruct((s, d), F32),
        compiler_params=_cparams(1),
        name="outproj",
    )(h, *ots, *[w.astype(BF16) for w in ws])


def _softmax_step_ref(s_ref, stream, vts, m_ref, acc_ref):
    m_old = m_ref[stream]
    m_new = jnp.maximum(m_old, jnp.max(s_ref[stream], axis=0, keepdims=True))
    pv = None
    for u in range(KC):
        p = jnp.exp2(s_ref[stream, u * TK:(u + 1) * TK, :] - m_new)
        d = _dot(vts[u], p.astype(BF16))
        pv = d if pv is None else pv + d
    acc_ref[stream] = jnp.exp2(m_old - m_new) * acc_ref[stream] + pv
    m_ref[stream] = m_new


def _with_ones(vt):
    return jnp.concatenate([vt, jnp.ones((ROWSUM_ROWS, vt.shape[1]), vt.dtype)], axis=0)


def _normalised(acc, dv):
    return acc[:dv] / acc[dv:dv + 1]


def _score_pipeline(i, n_chunks, k_ref, q_halves, bias_ref, heads, mask_fn, consume,
                    s_even, s_odd):
    def produce(near, c, s_ref):
        for u in range(KC):
            j = c * KC + u
            k0 = pl.multiple_of(j * TK, TK)
            kt = k_ref[pl.ds(k0, TK), :]
            dist = i - j
            which = jnp.where(dist < 0, 3, jnp.minimum(dist, 2))
            sel = None if mask_fn is None else mask_fn(k0)
            for a, qh in enumerate(q_halves):
                s = _dot(kt, qh)
                if near:
                    s = s + bias_ref[heads[a], which]
                s_ref[a, u * TK:(u + 1) * TK, :] = s if sel is None else jnp.where(sel, s, NEG)

    def pair(near, p, carry):
        c = 2 * p
        produce(near, c + 1, s_odd)
        consume(c, s_even)
        produce(near, jnp.minimum(c + 2, n_chunks - 1), s_even)
        consume(c + 1, s_odd)
        return carry

    n_far = jnp.maximum(i - 1, 0) // KC
    far_pairs = jnp.maximum(n_far - 1, 0) // 2
    produce(True, 0, s_even)
    lax.fori_loop(0, far_pairs, functools.partial(pair, False), 0)
    lax.fori_loop(far_pairs, n_chunks // 2, functools.partial(pair, True), 0)

    @pl.when(n_chunks % 2 == 1)
    def _last():
        consume(n_chunks - 1, s_even)


def _init_state(m_ref, acc_ref):
    m_ref[...] = jnp.full(m_ref.shape, NEG, F32)
    acc_ref[...] = jnp.zeros_like(acc_ref)


def _half_masked(qpair):
    row = lax.broadcasted_iota(I32, qpair.shape, 0)
    zero = jnp.zeros_like(qpair)
    return jnp.where(row < HEAD_DIM, qpair, zero), jnp.where(row >= HEAD_DIM, qpair, zero)


def _t5_bucket(rel):
    nb = T5_BUCKETS // 2
    max_exact = nb // 2
    offset = (rel < 0).astype(jnp.int32) * nb
    n = jnp.abs(rel)
    nf = jnp.maximum(n, 1).astype(jnp.float32)
    large = max_exact + (jnp.log(nf / max_exact) / math.log(T5_MAX_DIST / max_exact)
                         * (nb - max_exact)).astype(jnp.int32)
    large = jnp.minimum(large, nb - 1)
    return offset + jnp.where(n < max_exact, n, large)


def _toeplitz(u, n_keys, n_q):
    hn, length = u.shape
    w = jnp.concatenate([u[:, n_keys - 1:], u[:, :n_keys - 1]], axis=1)
    r = jnp.tile(w, (1, n_keys))[:, :n_keys * (length - 1)].reshape(hn, n_keys, length - 1)
    return r[:, :, :n_q]


def _t5_bias_tiles(table):
    table = table.astype(F32)
    far = table[_t5_bucket(jnp.full((1,), 2 * TK, jnp.int32))].T
    r = jnp.arange(-(TK - 1), TQ, dtype=jnp.int32)
    sk = jnp.arange(TK, dtype=jnp.int32)[:, None]
    tq = jnp.arange(TQ, dtype=jnp.int32)[None, :]
    tiles = []
    for dist in range(2):
        u = (table[_t5_bucket(dist * TK + r)].T - far) * LOG2E
        b = _toeplitz(u, TK, TQ)
        if dist == 0:
            b = jnp.where(((sk // CHUNK) <= (tq // CHUNK))[None], b, NEG)
        tiles.append(b)
    tiles.append(jnp.zeros_like(tiles[0]))
    tiles.append(jnp.full_like(tiles[0], NEG))
    return jnp.stack(tiles, axis=1)


def _key_to_f32(key):
    return lax.bitcast_convert_type(jnp.where(key < 0, INT_MIN - key, key), F32)


def _score_codes(x):
    b = lax.bitcast_convert_type(x, I32)
    b = jnp.where((b & 0x7FFFFFFF) < MIN_NORMAL_BITS, 0, b)
    key = jnp.where(b < 0, INT_MIN - b, b)
    fb = jnp.where(b < 0, b + 0xFFFF, b) & jnp.int32(-65536)
    return key, lax.bitcast_convert_type(fb, F32).astype(BF16)


def _tree_sum(x):
    while x.shape[0] > 1:
        half = x.shape[0] // 2
        x = x[:half] + x[half:]
    return x[0]


def _select_topk(keys_ref, hb_ref, thr_ref, n_chunks, top_k, idx_bits):
    rows = KC * TK
    one, zero = jnp.ones((), BF16), jnp.zeros((), BF16)

    def count16(cand):
        cand = jnp.where((cand > 0) & (cand < MIN_NORMAL_BITS), MIN_NORMAL_BITS, cand)
        tb = _key_to_f32(cand).astype(BF16)

        def body(c, acc):
            c0 = pl.multiple_of(c * rows, rows)
            ge = jnp.where(hb_ref[pl.ds(c0, rows), :] >= tb, one, zero)
            return acc + _tree_sum(ge.reshape(rows // 16, 16, TQ)).astype(F32)

        acc = lax.fori_loop(0, n_chunks, body, jnp.zeros((16, TQ), F32))
        return jnp.sum(acc, axis=0, keepdims=True).astype(I32)

    def count32(pred):
        def body(c, acc):
            c0 = pl.multiple_of(c * rows, rows)
            hit = pred(keys_ref[pl.ds(c0, rows), :], c0).astype(I32)
            return acc + jnp.sum(hit.reshape(rows // 8, 8, TQ), axis=0)

        acc = lax.fori_loop(0, n_chunks, body, jnp.zeros((8, TQ), I32))
        return jnp.sum(acc, axis=0, keepdims=True)

    def search(count, bits, thr, c_thr, n_fixed):
        def step(carry):
            b, thr, c_thr = carry
            cand = thr + lax.shift_left(jnp.int32(1), bits[0] - b)
            cnt = count(cand)
            ok = cnt >= top_k
            return b + 1, jnp.where(ok, cand, thr), jnp.where(ok, cnt, c_thr)

        def unresolved(carry):
            b, _, c_thr = carry
            return (b <= bits[0] - bits[1]) & (jnp.max(c_thr) > top_k)

        carry = lax.fori_loop(0, n_fixed, lambda _, c: step(c), (jnp.int32(0), thr, c_thr))
        return lax.while_loop(unresolved, step, carry)[1:]

    cnt = count16(jnp.zeros((1, TQ), I32))
    ok = cnt >= top_k
    thr = jnp.where(ok, 0, INT_MIN).astype(I32)
    c_thr = jnp.where(ok, cnt, 2 ** 30)
    thr, c_thr = search(count16, (30, 16), thr, c_thr, 15)
    thr, c_thr = search(lambda cand: count32(lambda keys, c0: keys >= cand), (15, 0), thr, c_thr, 8)
    thr_ref[...] = jnp.maximum(thr, INT_MIN + 1)

    split = (c_thr > top_k) & (thr > INT_MIN)

    @pl.when(jnp.max(split.astype(I32)) > 0)
    def _break_ties():
        need = top_k - count32(lambda keys, c0: keys > thr)

        def equal_before(limit):
            def pred(keys, c0):
                idx = c0 + lax.broadcasted_iota(I32, keys.shape, 0)
                return (keys == thr) & (idx < limit)
            return count32(pred)

        def step(b, q):
            cand = q + lax.shift_left(jnp.int32(1), idx_bits - 1 - b)
            return jnp.where(equal_before(cand) < need, cand, q)

        q = lax.fori_loop(0, idx_bits, step, jnp.zeros((1, TQ), I32))

        def lower(c, carry):
            c0 = pl.multiple_of(c * rows, rows)
            keys = keys_ref[pl.ds(c0, rows), :]
            idx = c0 + lax.broadcasted_iota(I32, keys.shape, 0)
            drop = split & (keys == thr) & (idx > q)
            keys_ref[pl.ds(c0, rows), :] = jnp.where(drop, thr - 1, keys)
            return carry

        lax.fori_loop(0, n_chunks, lower, 0)


def _dsa_body(top_k, idx_bits, iq_ref, iw_ref, ik_ref, q_ref, k_ref, vt_ref, bias_ref, o_ref,
              keys_ref, hb_ref, thr_ref, m_ref, acc_ref, s_even, s_odd):
    i = pl.program_id(0)
    g = pl.program_id(1)
    n_tiles = i + 1
    n_chunks = (i + KC) // KC

    @pl.when(g == 0)
    def _select():
        w_all = iw_ref[...]
        zpad = jnp.zeros((IDX_DIM, TQ), BF16)
        qz = [jnp.concatenate([iq_ref[h * IDX_DIM:(h + 1) * IDX_DIM, :], zpad], axis=0)
              for h in range(IDX_HEADS)]

        def score_tiles(p, carry):
            for u in range(2):
                k0 = pl.multiple_of((2 * p + u) * TK, TK)
                ikt = ik_ref[pl.ds(k0, TK), :]
                sc = jnp.zeros((TK, TQ), F32)
                for h in range(IDX_HEADS):
                    sc = sc + w_all[h:h + 1, :] * jnp.maximum(_dot(ikt, qz[h]), 0.0)
                keys_ref[pl.ds(k0, TK), :], hb_ref[pl.ds(k0, TK), :] = _score_codes(sc)
            return carry

        lax.fori_loop(0, (n_tiles + 1) // 2, score_tiles, 0)

        d0 = pl.multiple_of(i * TK, TK)
        sk = lax.broadcasted_iota(I32, (TK, TQ), 0) // CHUNK
        tq = lax.broadcasted_iota(I32, (TK, TQ), 1) // CHUNK
        no_key = jnp.full((TK, TQ), INT_MIN, I32)
        no_hb = jnp.full((TK, TQ), jnp.nan, BF16)
        keys_ref[pl.ds(d0, TK), :] = jnp.where(sk <= tq, keys_ref[pl.ds(d0, TK), :], no_key)
        hb_ref[pl.ds(d0, TK), :] = jnp.where(sk <= tq, hb_ref[pl.ds(d0, TK), :], no_hb)

        def fill_tile(j, carry):
            k0 = pl.multiple_of(j * TK, TK)
            keys_ref[pl.ds(k0, TK), :] = no_key
            hb_ref[pl.ds(k0, TK), :] = no_hb
            return carry

        lax.fori_loop(n_tiles, n_chunks * KC, fill_tile, 0)
        _select_topk(keys_ref, hb_ref, thr_ref, n_chunks, top_k, idx_bits)

    _init_state(m_ref, acc_ref)
    q_halves = _half_masked(q_ref[...])
    thr = thr_ref[...]

    def selected(k0):
        return keys_ref[pl.ds(k0, TK), :] >= thr

    def consume(c, s_ref):
        for a in range(2):
            vts = [_with_ones(vt_ref[c * KC + u, a * HEAD_DIM:(a + 1) * HEAD_DIM, :])
                   for u in range(KC)]
            _softmax_step_ref(s_ref, a, vts, m_ref, acc_ref)

    _score_pipeline(i, n_chunks, k_ref, q_halves, bias_ref, (0, 1), selected, consume,
                    s_even, s_odd)
    for a in range(2):
        o_ref[a * HEAD_DIM:(a + 1) * HEAD_DIM, :] = _normalised(acc_ref[a], HEAD_DIM)


def _dsa_attention(iq_t, iw_t, ik, q_t, k, v_t, bias, top_k):
    s = k.shape[0]
    n_pairs = A_HEADS // 2
    return pl.pallas_call(
        functools.partial(_dsa_body, top_k, (s - 1).bit_length()),
        grid=(s // TQ, n_pairs),
        in_specs=[
            pl.BlockSpec((IDX_HEADS * IDX_DIM, TQ), lambda i, g: (0, i)),
            pl.BlockSpec((IDX_HEADS, TQ), lambda i, g: (0, i)),
            pl.BlockSpec((s, 128), lambda i, g: (0, 0)),
            pl.BlockSpec((128, TQ), lambda i, g: (g, i)),
            pl.BlockSpec((s, 128), lambda i, g: (0, g)),
            pl.BlockSpec((s // TK, 128, TK), lambda i, g: (0, g, 0)),
            pl.BlockSpec((2, 4, TK, TQ), lambda i, g: (g, 0, 0, 0)),
        ],
        out_specs=pl.BlockSpec((128, TQ), lambda i, g: (g, i)),
        out_shape=jax.ShapeDtypeStruct((A_HEADS * HEAD_DIM, s), F32),
        scratch_shapes=[
            pltpu.VMEM((s, TQ), I32),
            pltpu.VMEM((s, TQ), BF16),
            pltpu.VMEM((1, TQ), I32),
            pltpu.VMEM((2, 1, TQ), F32),
            pltpu.VMEM((2, HEAD_DIM + ROWSUM_ROWS, TQ), F32),
            pltpu.VMEM((2, KC * TK, TQ), F32),
            pltpu.VMEM((2, KC * TK, TQ), F32),
        ],
        compiler_params=_cparams(2),
        name="dsa_attention",
    )(iq_t, iw_t, ik, q_t, k, v_t, bias)


def _diff_body(lambda_init, q_ref, k_ref, vt_ref, bias_ref, lam_ref, subln_ref, o_ref,
               m_ref, acc_ref, s_even, s_odd):
    i = pl.program_id(0)
    _init_state(m_ref, acc_ref)
    q_halves = _half_masked(q_ref[...])

    def consume(c, s_ref):
        vts = [_with_ones(vt_ref[c * KC + u]) for u in range(KC)]
        for a in range(2):
            _softmax_step_ref(s_ref, a, vts, m_ref, acc_ref)

    _score_pipeline(i, (i + KC) // KC, k_ref, q_halves, bias_ref, (0, 0), None, consume,
                    s_even, s_odd)

    lq1, lk1, lq2, lk2 = (lam_ref[r:r + 1, :] for r in range(4))
    lam = (jnp.exp(jnp.sum(lq1 * lk1, keepdims=True)) - jnp.exp(jnp.sum(lq2 * lk2, keepdims=True))
           + lambda_init)
    o = _normalised(acc_ref[0], B_VDIM) - lam * _normalised(acc_ref[1], B_VDIM)
    ms = jnp.mean(o * o, axis=0, keepdims=True)
    o_ref[...] = o * lax.rsqrt(ms + EPS) * subln_ref[...] * (1.0 - lambda_init)


def _diff_attention(q_t, k, v_t, bias, lam_rows, subln, lambda_init):
    s = k.shape[0]
    return pl.pallas_call(
        functools.partial(_diff_body, lambda_init),
        grid=(s // TQ, B_HEADS),
        in_specs=[
            pl.BlockSpec((128, TQ), lambda i, h: (h, i)),
            pl.BlockSpec((s, 128), lambda i, h: (0, h)),
            pl.BlockSpec((s // TK, B_VDIM, TK), lambda i, h: (0, h, 0)),
            pl.BlockSpec((1, 4, TK, TQ), lambda i, h: (h, 0, 0, 0)),
            pl.BlockSpec((4, HEAD_DIM), lambda i, h: (0, 0)),
            pl.BlockSpec((B_VDIM, 1), lambda i, h: (0, 0)),
        ],
        out_specs=pl.BlockSpec((B_VDIM, TQ), lambda i, h: (h, i)),
        out_shape=jax.ShapeDtypeStruct((B_HEADS * B_VDIM, s), F32),
        scratch_shapes=[
            pltpu.VMEM((2, 1, TQ), F32),
            pltpu.VMEM((2, B_VDIM + ROWSUM_ROWS, TQ), F32),
            pltpu.VMEM((2, KC * TK, TQ), F32),
            pltpu.VMEM((2, KC * TK, TQ), F32),
        ],
        compiler_params=_cparams(2),
        name="diff_attention",
    )(q_t, k, v_t, bias, lam_rows, subln.astype(F32).reshape(B_VDIM, 1))


def _band_bias_tiles(rel_bias):
    sk = jnp.arange(BAND_T, dtype=jnp.int32)[:, None]
    tq = jnp.arange(BAND_T, dtype=jnp.int32)[None, :]
    table = rel_bias.astype(F32)
    r = jnp.arange(-(BAND_T - 1), BAND_T, dtype=jnp.int32)
    tiles = []
    for blk in range(2):
        rel = (1 - blk) * BAND_T + r
        u = table[jnp.clip(rel, -REL_CLIP, REL_CLIP) + REL_CLIP].T
        kc = sk // CHUNK + blk * (BAND_T // CHUNK)
        qc = tq // CHUNK + BAND_T // CHUNK
        ok = (kc <= qc) & (kc >= qc - (C_BAND - 1))
        tiles.append(jnp.where(ok[None], _toeplitz(u * LOG2E, BAND_T, BAND_T), NEG))
    return jnp.stack(tiles, axis=1)


def _band_body(q_ref, kp_ref, kc_ref, vp_ref, vc_ref, bias_ref, o_ref):
    i = pl.program_id(1)
    q_halves = _half_masked(q_ref[...])
    for a in range(2):
        rows = slice(a * HEAD_DIM, (a + 1) * HEAD_DIM)
        s_prev = _dot(kp_ref[...], q_halves[a]) + jnp.where(i > 0, bias_ref[a, 0], NEG)
        s_cur = _dot(kc_ref[...], q_halves[a]) + bias_ref[a, 1]
        m = jnp.maximum(jnp.max(s_prev, axis=0, keepdims=True),
                        jnp.max(s_cur, axis=0, keepdims=True))
        acc = (_dot(_with_ones(vp_ref[0, rows, :]), jnp.exp2(s_prev - m).astype(BF16))
               + _dot(_with_ones(vc_ref[0, rows, :]), jnp.exp2(s_cur - m).astype(BF16)))
        o_ref[rows, :] = _normalised(acc, HEAD_DIM)


def _band_attention(q_t, k, v_t, bias):
    s = k.shape[0]
    t = BAND_T
    prev = lambda i: jnp.maximum(i - 1, 0)
    return pl.pallas_call(
        _band_body,
        grid=(C_HEADS // 2, s // t),
        in_specs=[
            pl.BlockSpec((128, t), lambda p, i: (p, i)),
            pl.BlockSpec((t, 128), lambda p, i: (prev(i), p)),
            pl.BlockSpec((t, 128), lambda p, i: (i, p)),
            pl.BlockSpec((1, 128, t), lambda p, i: (prev(i), p, 0)),
            pl.BlockSpec((1, 128, t), lambda p, i: (i, p, 0)),
            pl.BlockSpec((2, 2, t, t), lambda p, i: (p, 0, 0, 0)),
        ],
        out_specs=pl.BlockSpec((128, t), lambda p, i: (p, i)),
        out_shape=jax.ShapeDtypeStruct((C_HEADS * HEAD_DIM, s), F32),
        compiler_params=_cparams(2),
        name="band_attention",
    )(q_t, k, k, v_t, v_t, bias)


def _mem_body(q_ref, k_ref, vt_ref, o_ref):
    for h in range(M_HEADS):
        rows = slice(h * M_DIM, (h + 1) * M_DIM)
        s = _dot(k_ref[:, rows], q_ref[rows, :])
        p = jnp.exp2(s - jnp.max(s, axis=0, keepdims=True))
        denom = jnp.sum(p, axis=0, keepdims=True)
        o_ref[rows, :] = _dot(vt_ref[0, rows, :], p.astype(BF16)) / denom


def _mem_attention(q_t, k, v_t):
    f, s = q_t.shape
    tq = min(ROW_TILE, s)
    return pl.pallas_call(
        _mem_body,
        grid=(s // tq,),
        in_specs=[
            pl.BlockSpec((f, tq), lambda i: (0, i)),
            pl.BlockSpec(k.shape, lambda i: (0, 0)),
            pl.BlockSpec(v_t.shape, lambda i: (0, 0, 0)),
        ],
        out_specs=pl.BlockSpec((f, tq), lambda i: (0, i)),
        out_shape=jax.ShapeDtypeStruct((f, s), F32),
        compiler_params=_cparams(1),
        name="mem_attention",
    )(q_t, k, v_t)


def _even_mixer(h, mix_g, w_in, a_qg, a_kg, idx_kg, b_qg, b_kg, lq1, lk1, lq2, lk2, b_subln,
                w_out, t5_bias, layer_idx):
    s = h.shape[0]
    sizes = [A_HEADS * HEAD_DIM] * 3 + [IDX_HEADS * IDX_DIM, IDX_DIM, IDX_HEADS] + \
            [B_HEADS * 2 * HEAD_DIM] * 2 + [B_HEADS * B_VDIM]
    cuts = np.cumsum([0] + sizes)
    w = [w_in[:, cuts[n]:cuts[n + 1]] for n in range(len(sizes))]
    qk_scale = HEAD_DIM ** -0.5 * LOG2E
    segs = [
        _Seg(w[0], HEAD_DIM, a_qg, qk_scale, "fm"),
        _Seg(w[1], HEAD_DIM, a_kg, 1.0, "tm"),
        _Seg(w[2], layout="vt"),
        _Seg(w[3], scale=IDX_DIM ** -0.5, layout="fm"),
        _Seg(w[4], IDX_DIM, idx_kg, 1.0, "tm", pad=IDX_DIM),
        _Seg(w[5], scale=IDX_HEADS ** -0.5, layout="fm", dtype=F32),
        _Seg(w[6], HEAD_DIM, b_qg, qk_scale, "fm"),
        _Seg(w[7], HEAD_DIM, b_kg, 1.0, "tm"),
        _Seg(w[8], layout="vt"),
    ]
    aq_t, ak, av_t, iq_t, ik, iw_t, bq_t, bk, bv_t = _project(h, mix_g, segs)
    bias = _t5_bias_tiles(t5_bias)
    assert s % (KC * TK) == 0
    top_k = min(TOPK_MAX, s // 4)
    out_a = _dsa_attention(iq_t, iw_t, ik, aq_t, ak, av_t, bias[:A_HEADS], top_k)
    lambda_init = 0.8 - 0.6 * math.exp(-0.3 * layer_idx)
    lam_rows = jnp.stack([lq1, lk1, lq2, lk2]).astype(F32)
    out_b = _diff_attention(bq_t, bk, bv_t, bias[A_HEADS:], lam_rows, b_subln, lambda_init)
    na = A_HEADS * HEAD_DIM
    return _outproj(h, [out_a, out_b], [w_out[:na], w_out[na:]])


def _odd_mixer(h, mix_g, w_in, c_qg, c_kg, rel_bias, w_out):
    f = C_HEADS * HEAD_DIM
    segs = [
        _Seg(w_in[:, :f], HEAD_DIM, c_qg, HEAD_DIM ** -0.5 * LOG2E, "fm"),
        _Seg(w_in[:, f:2 * f], HEAD_DIM, c_kg, 1.0, "tm"),
        _Seg(w_in[:, 2 * f:], layout="vt", tile=BAND_T),
    ]
    q_t, k, v_t = _project(h, mix_g, segs)
    out = _band_attention(q_t, k, v_t, _band_bias_tiles(rel_bias))
    return _outproj(h, [out], [w_out])


def _memory_xattn(h, mem, mg, sg, wq, wkv, qg, kg, wo):
    f = M_HEADS * M_DIM
    n_mem = mem.shape[0]
    (q_t,) = _project(h, mg, [_Seg(wq, M_DIM, qg, M_DIM ** -0.5 * LOG2E, "fm")])
    k, v_t = _project(mem, sg, [_Seg(wkv[:, :f], M_DIM, kg, 1.0, "tm"),
                                _Seg(wkv[:, f:], layout="vt", tile=n_mem)])
    return _outproj(h, [_mem_attention(q_t, k, v_t)], [wo])


def kernel(x, mem, t5_bias,
           l0_ffn1_norm, l0_ffn1_wg, l0_ffn1_wu, l0_ffn1_wd,
           l0_mix_norm, l0_w_in, l0_a_q_norm, l0_a_k_norm, l0_idx_k_norm,
           l0_b_q_norm, l0_b_k_norm, l0_b_lq1, l0_b_lk1, l0_b_lq2, l0_b_lk2, l0_b_subln, l0_w_out,
           l0_mem_norm, l0_mem_src_norm, l0_mem_wq, l0_mem_wkv, l0_mem_q_norm, l0_mem_k_norm, l0_mem_wo,
           l0_ffn2_norm, l0_ffn2_wg, l0_ffn2_wu, l0_ffn2_wd,
           l1_ffn1_norm, l1_ffn1_wg, l1_ffn1_wu, l1_ffn1_wd,
           l1_mix_norm, l1_w_in, l1_c_q_norm, l1_c_k_norm, l1_c_rel_bias, l1_w_out,
           l1_mem_norm, l1_mem_src_norm, l1_mem_wq, l1_mem_wkv, l1_mem_q_norm, l1_mem_k_norm, l1_mem_wo,
           l1_ffn2_norm, l1_ffn2_wg, l1_ffn2_wu, l1_ffn2_wd):
    bsz, seq, d = x.shape
    assert bsz == 1 and mem.shape[0] == 1
    h = x.reshape(seq, d)
    m = mem.reshape(mem.shape[1], d)

    h = _ffn(h, l0_ffn1_norm, l0_ffn1_wg, l0_ffn1_wu, l0_ffn1_wd)
    h = _even_mixer(h, l0_mix_norm, l0_w_in, l0_a_q_norm, l0_a_k_norm, l0_idx_k_norm,
                    l0_b_q_norm, l0_b_k_norm, l0_b_lq1, l0_b_lk1, l0_b_lq2, l0_b_lk2,
                    l0_b_subln, l0_w_out, t5_bias, 0)
    h = _memory_xattn(h, m, l0_mem_norm, l0_mem_src_norm, l0_mem_wq, l0_mem_wkv,
                      l0_mem_q_norm, l0_mem_k_norm, l0_mem_wo)
    h = _ffn(h, l0_ffn2_norm, l0_ffn2_wg, l0_ffn2_wu, l0_ffn2_wd)

    h = _ffn(h, l1_ffn1_norm, l1_ffn1_wg, l1_ffn1_wu, l1_ffn1_wd)
    h = _odd_mixer(h, l1_mix_norm, l1_w_in, l1_c_q_norm, l1_c_k_norm, l1_c_rel_bias, l1_w_out)
    h = _memory_xattn(h, m, l1_mem_norm, l1_mem_src_norm, l1_mem_wq, l1_mem_wkv,
                      l1_mem_q_norm, l1_mem_k_norm, l1_mem_wo)
    h = _ffn(h, l1_ffn2_norm, l1_ffn2_wg, l1_ffn2_wu, l1_ffn2_wd)
    return h.reshape(bsz, seq, d)
```

```python
import functools
import math

import jax
import jax.numpy as jnp
import numpy as np
from jax import lax
from jax.experimental import pallas as pl
from jax.experimental.pallas import tpu as pltpu

D_MODEL = 1024
CHUNK = 64
HEAD_DIM = 64
EPS = 1e-6
A_HEADS = 8
IDX_HEADS = 8
IDX_DIM = 64
TOPK_MAX = 256
B_VDIM = 128
B_HEADS = 4
C_HEADS = 16
C_BAND = 9
REL_CLIP = 256
T5_BUCKETS = 32
T5_MAX_DIST = 128
M_HEADS = 4
M_DIM = 128
D_FF = 2816

F32 = jnp.float32
BF16 = jnp.bfloat16
I32 = jnp.int32

NEG = -1e30
LOG2E = math.log2(math.e)
INT_MIN = -(2 ** 31)
MIN_NORMAL_BITS = 0x00800000
VMEM_LIMIT = 60 * 1024 * 1024

ROW_TILE = 512
FF_CHUNK = 256
TQ = 256
TK = 256
KC = 4
ROWSUM_ROWS = 16
BAND_T = 512


def _cparams(n_axes):
    return pltpu.CompilerParams(dimension_semantics=("arbitrary",) * n_axes,
                                vmem_limit_bytes=VMEM_LIMIT)


def _dot(a, b):
    return jnp.dot(a, b, preferred_element_type=F32)


def _rms_rows(x, g):
    ms = jnp.mean(x * x, axis=-1, keepdims=True)
    return x * lax.rsqrt(ms + EPS) * g


def _ffn_body(h_ref, g_ref, wg_ref, wu_ref, wd_ref, o_ref, xn_ref, acc_ref):
    x = h_ref[...]
    xn_ref[...] = _rms_rows(x, g_ref[...]).astype(BF16)
    acc_ref[...] = jnp.zeros_like(acc_ref)

    def chunk(c, carry):
        xn = xn_ref[...]
        gate = _dot(xn, wg_ref[c])
        up = _dot(xn, wu_ref[c])
        act = (gate * jax.nn.sigmoid(gate) * up).astype(BF16)
        acc_ref[...] += _dot(act, wd_ref[c])
        return carry

    lax.fori_loop(0, wg_ref.shape[0], chunk, 0)
    o_ref[...] = x + 0.5 * acc_ref[...]


def _ffn(h, g, wg, wu, wd):
    s, d = h.shape
    ff = wg.shape[1]
    nc = ff // FF_CHUNK
    wg3 = wg.astype(BF16).reshape(d, nc, FF_CHUNK).transpose(1, 0, 2)
    wu3 = wu.astype(BF16).reshape(d, nc, FF_CHUNK).transpose(1, 0, 2)
    wd3 = wd.astype(BF16).reshape(nc, FF_CHUNK, d)
    tm = min(ROW_TILE, s)
    const3 = lambda i: (0, 0, 0)
    return pl.pallas_call(
        _ffn_body,
        grid=(s // tm,),
        in_specs=[
            pl.BlockSpec((tm, d), lambda i: (i, 0)),
            pl.BlockSpec((1, d), lambda i: (0, 0)),
            pl.BlockSpec((nc, d, FF_CHUNK), const3, pipeline_mode=pl.Buffered(1)),
            pl.BlockSpec((nc, d, FF_CHUNK), const3, pipeline_mode=pl.Buffered(1)),
            pl.BlockSpec((nc, FF_CHUNK, d), const3, pipeline_mode=pl.Buffered(1)),
        ],
        out_specs=pl.BlockSpec((tm, d), lambda i: (i, 0)),
        out_shape=jax.ShapeDtypeStruct((s, d), F32),
        scratch_shapes=[pltpu.VMEM((tm, d), BF16), pltpu.VMEM((tm, d), F32)],
        compiler_params=_cparams(1),
        name="ffn",
    )(h, g.reshape(1, d), wg3, wu3, wd3)


class _Seg:
    def __init__(self, w, head_dim=None, gain=None, scale=1.0, layout="fm",
                 dtype=BF16, tile=TK, pad=0):
        self.w = w
        self.head_dim = head_dim
        self.gain = gain
        self.scale = scale
        self.layout = layout
        self.dtype = dtype
        self.tile = tile
        self.pad = pad


def _proj_body(segs, x_ref, g_ref, *refs):
    n = len(segs)
    w_refs = refs[:n]
    gains = [r for r in refs[n:2 * n]]
    outs = refs[2 * n:]
    xn = _rms_rows(x_ref[...], g_ref[...]).astype(BF16)
    tm = xn.shape[0]
    for seg, w_ref, gain_ref, o_ref in zip(segs, w_refs, gains, outs):
        yt = lax.dot_general(w_ref[...], xn, (((1,), (1,)), ((), ())),
                             preferred_element_type=F32)
        f = yt.shape[0]
        if seg.head_dim is not None:
            hd = seg.head_dim
            y3 = yt.reshape(f // hd, hd, tm)
            ms = jnp.mean(y3 * y3, axis=1, keepdims=True)
            y3 = y3 * lax.rsqrt(ms + EPS) * gain_ref[...][None]
            yt = y3.reshape(f, tm)
        if seg.scale != 1.0:
            yt = yt * seg.scale
        if seg.pad:
            yt = jnp.concatenate([yt, jnp.zeros((seg.pad, tm), F32)], axis=0)
        if seg.layout == "fm":
            o_ref[...] = yt.astype(seg.dtype)
        elif seg.layout == "tm":
            o_ref[...] = yt.T.astype(seg.dtype)
        else:
            for t in range(tm // seg.tile):
                o_ref[t] = yt[:, t * seg.tile:(t + 1) * seg.tile].astype(seg.dtype)


def _project(x, g, segs):
    s, d = x.shape
    tm = min(ROW_TILE, s)
    in_specs = [pl.BlockSpec((tm, d), lambda i: (i, 0)),
                pl.BlockSpec((1, d), lambda i: (0, 0))]
    args = [x, g.reshape(1, d)]
    for seg in segs:
        wt = seg.w.T.astype(BF16)
        args.append(wt)
        in_specs.append(pl.BlockSpec(wt.shape, lambda i: (0, 0)))
    for seg in segs:
        hd = seg.head_dim or 8
        gain = seg.gain if seg.gain is not None else jnp.ones((hd,), F32)
        args.append(gain.astype(F32).reshape(hd, 1))
        in_specs.append(pl.BlockSpec((hd, 1), lambda i: (0, 0)))
    out_shapes, out_specs = [], []
    for seg in segs:
        f = seg.w.shape[1] + seg.pad
        if seg.layout == "fm":
            out_shapes.append(jax.ShapeDtypeStruct((f, s), seg.dtype))
            out_specs.append(pl.BlockSpec((f, tm), lambda i: (0, i)))
        elif seg.layout == "tm":
            out_shapes.append(jax.ShapeDtypeStruct((s, f), seg.dtype))
            out_specs.append(pl.BlockSpec((tm, f), lambda i: (i, 0)))
        else:
            nt = tm // seg.tile
            out_shapes.append(jax.ShapeDtypeStruct((s // seg.tile, f, seg.tile), seg.dtype))
            out_specs.append(pl.BlockSpec((nt, f, seg.tile), lambda i: (i, 0, 0)))
    return pl.pallas_call(
        functools.partial(_proj_body, segs),
        grid=(s // tm,),
        in_specs=in_specs,
        out_specs=out_specs,
        out_shape=out_shapes,
        compiler_params=_cparams(1),
        name="project",
    )(*args)


def _mix_out_mem_body(n, h_ref, *refs):
    ot_refs, w_refs = refs[:n], refs[n:2 * n]
    g_ref, wq_ref, qg_ref, k_ref, vt_ref, wo_ref, o_ref = refs[2 * n:]
    h1 = h_ref[...]
    for ot_ref, w_ref in zip(ot_refs, w_refs):
        h1 = h1 + _dot(ot_ref[...].T.astype(BF16), w_ref[...])
    tm = h1.shape[0]

    xn = _rms_rows(h1, g_ref[...]).astype(BF16)
    qt = lax.dot_general(wq_ref[...], xn, (((1,), (1,)), ((), ())), preferred_element_type=F32)
    q3 = qt.reshape(M_HEADS, M_DIM, tm)
    ms = jnp.mean(q3 * q3, axis=1, keepdims=True)
    q3 = q3 * lax.rsqrt(ms + EPS) * qg_ref[...][None] * (M_DIM ** -0.5 * LOG2E)
    q = q3.reshape(M_HEADS * M_DIM, tm).astype(BF16)

    outs = []
    for hd in range(M_HEADS):
        rows = slice(hd * M_DIM, (hd + 1) * M_DIM)
        s = _dot(k_ref[:, rows], q[rows, :])
        p = jnp.exp2(s - jnp.max(s, axis=0, keepdims=True))
        outs.append(_normalised(_dot(_with_ones(vt_ref[0, rows, :]), p.astype(BF16)), M_DIM))
    o = jnp.concatenate(outs, axis=0)
    o_ref[...] = h1 + _dot(o.T.astype(BF16), wo_ref[...])


def _mix_out_mem(h, ots, ws, g, wq, qg, k, v_t, wo):
    s, d = h.shape
    tm = min(ROW_TILE, s)
    n = len(ots)
    const2 = lambda i: (0, 0)
    in_specs = [pl.BlockSpec((tm, d), lambda i: (i, 0))]
    in_specs += [pl.BlockSpec((ot.shape[0], tm), lambda i: (0, i)) for ot in ots]
    in_specs += [pl.BlockSpec(w.shape, const2) for w in ws]
    in_specs += [
        pl.BlockSpec((1, d), const2),
        pl.BlockSpec((wq.shape[1], d), const2),
        pl.BlockSpec((M_DIM, 1), const2),
        pl.BlockSpec(k.shape, const2),
        pl.BlockSpec(v_t.shape, lambda i: (0, 0, 0)),
        pl.BlockSpec(wo.shape, const2),
    ]
    return pl.pallas_call(
        functools.partial(_mix_out_mem_body, n),
        grid=(s // tm,),
        in_specs=in_specs,
        out_specs=pl.BlockSpec((tm, d), lambda i: (i, 0)),
        out_shape=jax.ShapeDtypeStruct((s, d), F32),
        compiler_params=_cparams(1),
        name="mix_out_mem",
    )(h, *ots, *[w.astype(BF16) for w in ws], g.reshape(1, d), wq.T.astype(BF16),
      qg.astype(F32).reshape(M_DIM, 1), k, v_t, wo.astype(BF16))


def _softmax_step_ref(s_ref, stream, vts, m_ref, acc_ref):
    m_old = m_ref[stream]
    m_new = jnp.maximum(m_old, jnp.max(s_ref[stream], axis=0, keepdims=True))
    pv = None
    for u in range(KC):
        p = jnp.exp2(s_ref[stream, u * TK:(u + 1) * TK, :] - m_new)
        d = _dot(vts[u], p.astype(BF16))
        pv = d if pv is None else pv + d
    acc_ref[stream] = jnp.exp2(m_old - m_new) * acc_ref[stream] + pv
    m_ref[stream] = m_new


def _with_ones(vt):
    return jnp.concatenate([vt, jnp.ones((ROWSUM_ROWS, vt.shape[1]), vt.dtype)], axis=0)


def _normalised(acc, dv):
    return acc[:dv] / acc[dv:dv + 1]


def _score_pipeline(i, n_chunks, k_ref, q_halves, bias_ref, heads, mask_fn, consume,
                    s_even, s_odd):
    def produce(near, c, s_ref):
        for u in range(KC):
            j = c * KC + u
            k0 = pl.multiple_of(j * TK, TK)
            kt = k_ref[pl.ds(k0, TK), :]
            dist = i - j
            which = jnp.where(dist < 0, 3, jnp.minimum(dist, 2))
            sel = None if mask_fn is None else mask_fn(k0)
            for a, qh in enumerate(q_halves):
                s = _dot(kt, qh)
                if near:
                    s = s + bias_ref[heads[a], which]
                s_ref[a, u * TK:(u + 1) * TK, :] = s if sel is None else jnp.where(sel, s, NEG)

    def pair(near, p, carry):
        c = 2 * p
        produce(near, c + 1, s_odd)
        consume(c, s_even)
        produce(near, jnp.minimum(c + 2, n_chunks - 1), s_even)
        consume(c + 1, s_odd)
        return carry

    n_far = jnp.maximum(i - 1, 0) // KC
    far_pairs = jnp.maximum(n_far - 1, 0) // 2
    produce(True, 0, s_even)
    lax.fori_loop(0, far_pairs, functools.partial(pair, False), 0)
    lax.fori_loop(far_pairs, n_chunks // 2, functools.partial(pair, True), 0)

    @pl.when(n_chunks % 2 == 1)
    def _last():
        consume(n_chunks - 1, s_even)


def _init_state(m_ref, acc_ref):
    m_ref[...] = jnp.full(m_ref.shape, NEG, F32)
    acc_ref[...] = jnp.zeros_like(acc_ref)


def _half_masked(qpair):
    row = lax.broadcasted_iota(I32, qpair.shape, 0)
    zero = jnp.zeros_like(qpair)
    return jnp.where(row < HEAD_DIM, qpair, zero), jnp.where(row >= HEAD_DIM, qpair, zero)


def _t5_bucket(rel):
    nb = T5_BUCKETS // 2
    max_exact = nb // 2
    offset = (rel < 0).astype(jnp.int32) * nb
    n = jnp.abs(rel)
    nf = jnp.maximum(n, 1).astype(jnp.float32)
    large = max_exact + (jnp.log(nf / max_exact) / math.log(T5_MAX_DIST / max_exact)
                         * (nb - max_exact)).astype(jnp.int32)
    large = jnp.minimum(large, nb - 1)
    return offset + jnp.where(n < max_exact, n, large)


def _toeplitz_body(masks, fills, n_keys, n_q, w_ref, o_ref):
    sk = lax.broadcasted_iota(I32, (n_keys, n_q), 0)
    tq = lax.broadcasted_iota(I32, (n_keys, n_q), 1)
    for blk, mask in enumerate(masks):
        w = jnp.broadcast_to(w_ref[0, blk], (n_keys, w_ref.shape[-1]))
        t = pltpu.roll(w, 0, 1, stride=1, stride_axis=0)[:, :n_q]
        o_ref[0, blk] = t if mask is None else jnp.where(mask(sk, tq), t, NEG)
    for n, fill in enumerate(fills):
        o_ref[0, len(masks) + n] = jnp.full((n_keys, n_q), fill, F32)


def _toeplitz_tiles(u, n_keys, n_q, masks, fills=()):
    hn, nb, _ = u.shape
    width = pl.cdiv(n_keys + n_q - 1, 128) * 128
    w = jnp.concatenate([u[..., n_keys - 1:],
                         jnp.zeros((hn, nb, width - (n_keys + n_q - 1)), F32),
                         u[..., :n_keys - 1]], axis=-1)
    n_out = nb + len(fills)
    return pl.pallas_call(
        functools.partial(_toeplitz_body, tuple(masks), tuple(fills), n_keys, n_q),
        grid=(hn,),
        in_specs=[pl.BlockSpec((1, nb, 1, width), lambda h: (h, 0, 0, 0))],
        out_specs=pl.BlockSpec((1, n_out, n_keys, n_q), lambda h: (h, 0, 0, 0)),
        out_shape=jax.ShapeDtypeStruct((hn, n_out, n_keys, n_q), F32),
        compiler_params=_cparams(1),
        name="toeplitz_tiles",
    )(w.reshape(hn, nb, 1, width))


def _t5_bias_tiles(table):
    table = table.astype(F32)
    far = table[_t5_bucket(jnp.full((1,), 2 * TK, jnp.int32))].T
    r = jnp.arange(-(TK - 1), TQ, dtype=jnp.int32)
    u = jnp.stack([(table[_t5_bucket(dist * TK + r)].T - far) * LOG2E for dist in range(2)],
                  axis=1)
    chunk_causal = lambda sk, tq: (sk // CHUNK) <= (tq // CHUNK)
    return _toeplitz_tiles(u, TK, TQ, (chunk_causal, None), fills=(0.0, NEG))


def _key_to_f32(key):
    return lax.bitcast_convert_type(jnp.where(key < 0, INT_MIN - key, key), F32)


def _score_codes(x):
    b = lax.bitcast_convert_type(x, I32)
    b = jnp.where((b & 0x7FFFFFFF) < MIN_NORMAL_BITS, 0, b)
    key = jnp.where(b < 0, INT_MIN - b, b)
    fb = jnp.where(b < 0, b + 0xFFFF, b) & jnp.int32(-65536)
    return key, lax.bitcast_convert_type(fb, F32).astype(BF16)


def _tree_sum(x):
    while x.shape[0] > 1:
        half = x.shape[0] // 2
        x = x[:half] + x[half:]
    return x[0]


def _select_topk(keys_ref, hb_ref, thr_ref, n_chunks, top_k, idx_bits):
    rows = KC * TK
    one, zero = jnp.ones((), BF16), jnp.zeros((), BF16)

    def count16(cand):
        cand = jnp.where((cand > 0) & (cand < MIN_NORMAL_BITS), MIN_NORMAL_BITS, cand)
        tb = _key_to_f32(cand).astype(BF16)

        def body(c, acc):
            c0 = pl.multiple_of(c * rows, rows)
            ge = jnp.where(hb_ref[pl.ds(c0, rows), :] >= tb, one, zero)
            return acc + _tree_sum(ge.reshape(rows // 16, 16, TQ)).astype(F32)

        acc = lax.fori_loop(0, n_chunks, body, jnp.zeros((16, TQ), F32))
        return jnp.sum(acc, axis=0, keepdims=True).astype(I32)

    def count32(pred):
        def body(c, acc):
            c0 = pl.multiple_of(c * rows, rows)
            hit = pred(keys_ref[pl.ds(c0, rows), :], c0).astype(I32)
            return acc + jnp.sum(hit.reshape(rows // 8, 8, TQ), axis=0)

        acc = lax.fori_loop(0, n_chunks, body, jnp.zeros((8, TQ), I32))
        return jnp.sum(acc, axis=0, keepdims=True)

    def search(count, bits, thr, c_thr, n_fixed):
        def step(carry):
            b, thr, c_thr = carry
            cand = thr + lax.shift_left(jnp.int32(1), bits[0] - b)
            cnt = count(cand)
            ok = cnt >= top_k
            return b + 1, jnp.where(ok, cand, thr), jnp.where(ok, cnt, c_thr)

        def unresolved(carry):
            b, _, c_thr = carry
            return (b <= bits[0] - bits[1]) & (jnp.max(c_thr) > top_k)

        carry = lax.fori_loop(0, n_fixed, lambda _, c: step(c), (jnp.int32(0), thr, c_thr))
        return lax.while_loop(unresolved, step, carry)[1:]

    cnt = count16(jnp.zeros((1, TQ), I32))
    ok = cnt >= top_k
    thr = jnp.where(ok, 0, INT_MIN).astype(I32)
    c_thr = jnp.where(ok, cnt, 2 ** 30)
    thr, c_thr = search(count16, (30, 16), thr, c_thr, 15)
    thr, c_thr = search(lambda cand: count32(lambda keys, c0: keys >= cand), (15, 0), thr, c_thr, 8)
    thr_ref[...] = jnp.maximum(thr, INT_MIN + 1)

    split = (c_thr > top_k) & (thr > INT_MIN)

    @pl.when(jnp.max(split.astype(I32)) > 0)
    def _break_ties():
        need = top_k - count32(lambda keys, c0: keys > thr)

        def equal_before(limit):
            def pred(keys, c0):
                idx = c0 + lax.broadcasted_iota(I32, keys.shape, 0)
                return (keys == thr) & (idx < limit)
            return count32(pred)

        def step(b, q):
            cand = q + lax.shift_left(jnp.int32(1), idx_bits - 1 - b)
            return jnp.where(equal_before(cand) < need, cand, q)

        q = lax.fori_loop(0, idx_bits, step, jnp.zeros((1, TQ), I32))

        def lower(c, carry):
            c0 = pl.multiple_of(c * rows, rows)
            keys = keys_ref[pl.ds(c0, rows), :]
            idx = c0 + lax.broadcasted_iota(I32, keys.shape, 0)
            drop = split & (keys == thr) & (idx > q)
            keys_ref[pl.ds(c0, rows), :] = jnp.where(drop, thr - 1, keys)
            return carry

        lax.fori_loop(0, n_chunks, lower, 0)


def _dsa_body(top_k, idx_bits, iq_ref, iw_ref, ik_ref, q_ref, k_ref, vt_ref, bias_ref, o_ref,
              keys_ref, hb_ref, thr_ref, m_ref, acc_ref, s_even, s_odd):
    i = pl.program_id(0)
    g = pl.program_id(1)
    n_tiles = i + 1
    n_chunks = (i + KC) // KC

    @pl.when(g == 0)
    def _select():
        w_all = iw_ref[...]
        zpad = jnp.zeros((IDX_DIM, TQ), BF16)
        qz = [jnp.concatenate([iq_ref[h * IDX_DIM:(h + 1) * IDX_DIM, :], zpad], axis=0)
              for h in range(IDX_HEADS)]

        def score_tiles(p, carry):
            for u in range(2):
                k0 = pl.multiple_of((2 * p + u) * TK, TK)
                ikt = ik_ref[pl.ds(k0, TK), :]
                sc = jnp.zeros((TK, TQ), F32)
                for h in range(IDX_HEADS):
                    sc = sc + w_all[h:h + 1, :] * jnp.maximum(_dot(ikt, qz[h]), 0.0)
                keys_ref[pl.ds(k0, TK), :], hb_ref[pl.ds(k0, TK), :] = _score_codes(sc)
            return carry

        lax.fori_loop(0, (n_tiles + 1) // 2, score_tiles, 0)

        d0 = pl.multiple_of(i * TK, TK)
        sk = lax.broadcasted_iota(I32, (TK, TQ), 0) // CHUNK
        tq = lax.broadcasted_iota(I32, (TK, TQ), 1) // CHUNK
        no_key = jnp.full((TK, TQ), INT_MIN, I32)
        no_hb = jnp.full((TK, TQ), jnp.nan, BF16)
        keys_ref[pl.ds(d0, TK), :] = jnp.where(sk <= tq, keys_ref[pl.ds(d0, TK), :], no_key)
        hb_ref[pl.ds(d0, TK), :] = jnp.where(sk <= tq, hb_ref[pl.ds(d0, TK), :], no_hb)

        def fill_tile(j, carry):
            k0 = pl.multiple_of(j * TK, TK)
            keys_ref[pl.ds(k0, TK), :] = no_key
            hb_ref[pl.ds(k0, TK), :] = no_hb
            return carry

        lax.fori_loop(n_tiles, n_chunks * KC, fill_tile, 0)
        _select_topk(keys_ref, hb_ref, thr_ref, n_chunks, top_k, idx_bits)

    _init_state(m_ref, acc_ref)
    q_halves = _half_masked(q_ref[...])
    thr = thr_ref[...]

    def selected(k0):
        return keys_ref[pl.ds(k0, TK), :] >= thr

    def consume(c, s_ref):
        for a in range(2):
            vts = [_with_ones(vt_ref[c * KC + u, a * HEAD_DIM:(a + 1) * HEAD_DIM, :])
                   for u in range(KC)]
            _softmax_step_ref(s_ref, a, vts, m_ref, acc_ref)

    _score_pipeline(i, n_chunks, k_ref, q_halves, bias_ref, (0, 1), selected, consume,
                    s_even, s_odd)
    for a in range(2):
        o_ref[a * HEAD_DIM:(a + 1) * HEAD_DIM, :] = _normalised(acc_ref[a], HEAD_DIM)


def _dsa_attention(iq_t, iw_t, ik, q_t, k, v_t, bias, top_k):
    s = k.shape[0]
    n_pairs = A_HEADS // 2
    return pl.pallas_call(
        functools.partial(_dsa_body, top_k, (s - 1).bit_length()),
        grid=(s // TQ, n_pairs),
        in_specs=[
            pl.BlockSpec((IDX_HEADS * IDX_DIM, TQ), lambda i, g: (0, i)),
            pl.BlockSpec((IDX_HEADS, TQ), lambda i, g: (0, i)),
            pl.BlockSpec((s, 128), lambda i, g: (0, 0)),
            pl.BlockSpec((128, TQ), lambda i, g: (g, i)),
            pl.BlockSpec((s, 128), lambda i, g: (0, g)),
            pl.BlockSpec((s // TK, 128, TK), lambda i, g: (0, g, 0)),
            pl.BlockSpec((2, 4, TK, TQ), lambda i, g: (g, 0, 0, 0)),
        ],
        out_specs=pl.BlockSpec((128, TQ), lambda i, g: (g, i)),
        out_shape=jax.ShapeDtypeStruct((A_HEADS * HEAD_DIM, s), F32),
        scratch_shapes=[
            pltpu.VMEM((s, TQ), I32),
            pltpu.VMEM((s, TQ), BF16),
            pltpu.VMEM((1, TQ), I32),
            pltpu.VMEM((2, 1, TQ), F32),
            pltpu.VMEM((2, HEAD_DIM + ROWSUM_ROWS, TQ), F32),
            pltpu.VMEM((2, KC * TK, TQ), F32),
            pltpu.VMEM((2, KC * TK, TQ), F32),
        ],
        compiler_params=_cparams(2),
        name="dsa_attention",
    )(iq_t, iw_t, ik, q_t, k, v_t, bias)


def _diff_body(lambda_init, q_ref, k_ref, vt_ref, bias_ref, lam_ref, subln_ref, o_ref,
               m_ref, acc_ref, s_even, s_odd):
    i = pl.program_id(0)
    _init_state(m_ref, acc_ref)
    q_halves = _half_masked(q_ref[...])

    def consume(c, s_ref):
        vts = [_with_ones(vt_ref[c * KC + u]) for u in range(KC)]
        for a in range(2):
            _softmax_step_ref(s_ref, a, vts, m_ref, acc_ref)

    _score_pipeline(i, (i + KC) // KC, k_ref, q_halves, bias_ref, (0, 0), None, consume,
                    s_even, s_odd)

    lq1, lk1, lq2, lk2 = (lam_ref[r:r + 1, :] for r in range(4))
    lam = (jnp.exp(jnp.sum(lq1 * lk1, keepdims=True)) - jnp.exp(jnp.sum(lq2 * lk2, keepdims=True))
           + lambda_init)
    o = _normalised(acc_ref[0], B_VDIM) - lam * _normalised(acc_ref[1], B_VDIM)
    ms = jnp.mean(o * o, axis=0, keepdims=True)
    o_ref[...] = o * lax.rsqrt(ms + EPS) * subln_ref[...] * (1.0 - lambda_init)


def _diff_attention(q_t, k, v_t, bias, lam_rows, subln, lambda_init):
    s = k.shape[0]
    return pl.pallas_call(
        functools.partial(_diff_body, lambda_init),
        grid=(s // TQ, B_HEADS),
        in_specs=[
            pl.BlockSpec((128, TQ), lambda i, h: (h, i)),
            pl.BlockSpec((s, 128), lambda i, h: (0, h)),
            pl.BlockSpec((s // TK, B_VDIM, TK), lambda i, h: (0, h, 0)),
            pl.BlockSpec((1, 4, TK, TQ), lambda i, h: (h, 0, 0, 0)),
            pl.BlockSpec((4, HEAD_DIM), lambda i, h: (0, 0)),
            pl.BlockSpec((B_VDIM, 1), lambda i, h: (0, 0)),
        ],
        out_specs=pl.BlockSpec((B_VDIM, TQ), lambda i, h: (h, i)),
        out_shape=jax.ShapeDtypeStruct((B_HEADS * B_VDIM, s), F32),
        scratch_shapes=[
            pltpu.VMEM((2, 1, TQ), F32),
            pltpu.VMEM((2, B_VDIM + ROWSUM_ROWS, TQ), F32),
            pltpu.VMEM((2, KC * TK, TQ), F32),
            pltpu.VMEM((2, KC * TK, TQ), F32),
        ],
        compiler_params=_cparams(2),
        name="diff_attention",
    )(q_t, k, v_t, bias, lam_rows, subln.astype(F32).reshape(B_VDIM, 1))


def _band_bias_tiles(rel_bias):
    table = rel_bias.astype(F32) * LOG2E
    r = jnp.arange(-(BAND_T - 1), BAND_T, dtype=jnp.int32)
    u = jnp.stack([table[jnp.clip((1 - blk) * BAND_T + r, -REL_CLIP, REL_CLIP) + REL_CLIP].T
                   for blk in range(2)], axis=1)

    def in_band(blk):
        def mask(sk, tq):
            kc = sk // CHUNK + blk * (BAND_T // CHUNK)
            qc = tq // CHUNK + BAND_T // CHUNK
            return (kc <= qc) & (kc >= qc - (C_BAND - 1))
        return mask

    return _toeplitz_tiles(u, BAND_T, BAND_T, (in_band(0), in_band(1)))


def _band_body(q_ref, kp_ref, kc_ref, vp_ref, vc_ref, bias_ref, o_ref):
    i = pl.program_id(1)
    q_halves = _half_masked(q_ref[...])
    for a in range(2):
        rows = slice(a * HEAD_DIM, (a + 1) * HEAD_DIM)
        s_prev = _dot(kp_ref[...], q_halves[a]) + jnp.where(i > 0, bias_ref[a, 0], NEG)
        s_cur = _dot(kc_ref[...], q_halves[a]) + bias_ref[a, 1]
        m = jnp.maximum(jnp.max(s_prev, axis=0, keepdims=True),
                        jnp.max(s_cur, axis=0, keepdims=True))
        acc = (_dot(_with_ones(vp_ref[0, rows, :]), jnp.exp2(s_prev - m).astype(BF16))
               + _dot(_with_ones(vc_ref[0, rows, :]), jnp.exp2(s_cur - m).astype(BF16)))
        o_ref[rows, :] = _normalised(acc, HEAD_DIM)


def _band_attention(q_t, k, v_t, bias):
    s = k.shape[0]
    t = BAND_T
    prev = lambda i: jnp.maximum(i - 1, 0)
    return pl.pallas_call(
        _band_body,
        grid=(C_HEADS // 2, s // t),
        in_specs=[
            pl.BlockSpec((128, t), lambda p, i: (p, i)),
            pl.BlockSpec((t, 128), lambda p, i: (prev(i), p)),
            pl.BlockSpec((t, 128), lambda p, i: (i, p)),
            pl.BlockSpec((1, 128, t), lambda p, i: (prev(i), p, 0)),
            pl.BlockSpec((1, 128, t), lambda p, i: (i, p, 0)),
            pl.BlockSpec((2, 2, t, t), lambda p, i: (p, 0, 0, 0)),
        ],
        out_specs=pl.BlockSpec((128, t), lambda p, i: (p, i)),
        out_shape=jax.ShapeDtypeStruct((C_HEADS * HEAD_DIM, s), F32),
        compiler_params=_cparams(2),
        name="band_attention",
    )(q_t, k, k, v_t, v_t, bias)


def _even_mixer(h, mix_g, w_in, a_qg, a_kg, idx_kg, b_qg, b_kg, lq1, lk1, lq2, lk2, b_subln,
                w_out, t5_bias, layer_idx):
    s = h.shape[0]
    sizes = [A_HEADS * HEAD_DIM] * 3 + [IDX_HEADS * IDX_DIM, IDX_DIM, IDX_HEADS] + \
            [B_HEADS * 2 * HEAD_DIM] * 2 + [B_HEADS * B_VDIM]
    cuts = np.cumsum([0] + sizes)
    w = [w_in[:, cuts[n]:cuts[n + 1]] for n in range(len(sizes))]
    qk_scale = HEAD_DIM ** -0.5 * LOG2E
    segs = [
        _Seg(w[0], HEAD_DIM, a_qg, qk_scale, "fm"),
        _Seg(w[1], HEAD_DIM, a_kg, 1.0, "tm"),
        _Seg(w[2], layout="vt"),
        _Seg(w[3], scale=IDX_DIM ** -0.5, layout="fm"),
        _Seg(w[4], IDX_DIM, idx_kg, 1.0, "tm", pad=IDX_DIM),
        _Seg(w[5], scale=IDX_HEADS ** -0.5, layout="fm", dtype=F32),
        _Seg(w[6], HEAD_DIM, b_qg, qk_scale, "fm"),
        _Seg(w[7], HEAD_DIM, b_kg, 1.0, "tm"),
        _Seg(w[8], layout="vt"),
    ]
    aq_t, ak, av_t, iq_t, ik, iw_t, bq_t, bk, bv_t = _project(h, mix_g, segs)
    bias = _t5_bias_tiles(t5_bias)
    assert s % (KC * TK) == 0
    top_k = min(TOPK_MAX, s // 4)
    out_a = _dsa_attention(iq_t, iw_t, ik, aq_t, ak, av_t, bias[:A_HEADS], top_k)
    lambda_init = 0.8 - 0.6 * math.exp(-0.3 * layer_idx)
    lam_rows = jnp.stack([lq1, lk1, lq2, lk2]).astype(F32)
    out_b = _diff_attention(bq_t, bk, bv_t, bias[A_HEADS:], lam_rows, b_subln, lambda_init)
    na = A_HEADS * HEAD_DIM
    return [out_a, out_b], [w_out[:na], w_out[na:]]


def _odd_mixer(h, mix_g, w_in, c_qg, c_kg, rel_bias, w_out):
    f = C_HEADS * HEAD_DIM
    segs = [
        _Seg(w_in[:, :f], HEAD_DIM, c_qg, HEAD_DIM ** -0.5 * LOG2E, "fm"),
        _Seg(w_in[:, f:2 * f], HEAD_DIM, c_kg, 1.0, "tm"),
        _Seg(w_in[:, 2 * f:], layout="vt", tile=BAND_T),
    ]
    q_t, k, v_t = _project(h, mix_g, segs)
    out = _band_attention(q_t, k, v_t, _band_bias_tiles(rel_bias))
    return [out], [w_out]


def _mixer_out_and_memory(h, mixer_out, mem, mg, sg, wq, wkv, qg, kg, wo):
    ots, ws = mixer_out
    f = M_HEADS * M_DIM
    k, v_t = _project(mem, sg, [_Seg(wkv[:, :f], M_DIM, kg, 1.0, "tm"),
                                _Seg(wkv[:, f:], layout="vt", tile=mem.shape[0])])
    return _mix_out_mem(h, ots, ws, mg, wq, qg, k, v_t, wo)


def kernel(x, mem, t5_bias,
           l0_ffn1_norm, l0_ffn1_wg, l0_ffn1_wu, l0_ffn1_wd,
           l0_mix_norm, l0_w_in, l0_a_q_norm, l0_a_k_norm, l0_idx_k_norm,
           l0_b_q_norm, l0_b_k_norm, l0_b_lq1, l0_b_lk1, l0_b_lq2, l0_b_lk2, l0_b_subln, l0_w_out,
           l0_mem_norm, l0_mem_src_norm, l0_mem_wq, l0_mem_wkv, l0_mem_q_norm, l0_mem_k_norm, l0_mem_wo,
           l0_ffn2_norm, l0_ffn2_wg, l0_ffn2_wu, l0_ffn2_wd,
           l1_ffn1_norm, l1_ffn1_wg, l1_ffn1_wu, l1_ffn1_wd,
           l1_mix_norm, l1_w_in, l1_c_q_norm, l1_c_k_norm, l1_c_rel_bias, l1_w_out,
           l1_mem_norm, l1_mem_src_norm, l1_mem_wq, l1_mem_wkv, l1_mem_q_norm, l1_mem_k_norm, l1_mem_wo,
           l1_ffn2_norm, l1_ffn2_wg, l1_ffn2_wu, l1_ffn2_wd):
    bsz, seq, d = x.shape
    assert bsz == 1 and mem.shape[0] == 1
    h = x.reshape(seq, d)
    m = mem.reshape(mem.shape[1], d)

    h = _ffn(h, l0_ffn1_norm, l0_ffn1_wg, l0_ffn1_wu, l0_ffn1_wd)
    mixed = _even_mixer(h, l0_mix_norm, l0_w_in, l0_a_q_norm, l0_a_k_norm, l0_idx_k_norm,
                        l0_b_q_norm, l0_b_k_norm, l0_b_lq1, l0_b_lk1, l0_b_lq2, l0_b_lk2,
                        l0_b_subln, l0_w_out, t5_bias, 0)
    h = _mixer_out_and_memory(h, mixed, m, l0_mem_norm, l0_mem_src_norm, l0_mem_wq, l0_mem_wkv,
                              l0_mem_q_norm, l0_mem_k_norm, l0_mem_wo)
    h = _ffn(h, l0_ffn2_norm, l0_ffn2_wg, l0_ffn2_wu, l0_ffn2_wd)

    h = _ffn(h, l1_ffn1_norm, l1_ffn1_wg, l1_ffn1_wu, l1_ffn1_wd)
    mixed = _odd_mixer(h, l1_mix_norm, l1_w_in, l1_c_q_norm, l1_c_k_norm, l1_c_rel_bias, l1_w_out)
    h = _mixer_out_and_memory(h, mixed, m, l1_mem_norm, l1_mem_src_norm, l1_mem_wq, l1_mem_wkv,
                              l1_mem_q_norm, l1_mem_k_norm, l1_mem_wo)
    h = _ffn(h, l1_ffn2_norm, l1_ffn2_wg, l1_ffn2_wu, l1_ffn2_wd)
    return h.reshape(bsz, seq, d)
```

```python
import functools
import math

import jax
import jax.numpy as jnp
import numpy as np
from jax import lax
from jax.experimental import pallas as pl
from jax.experimental.pallas import tpu as pltpu

D_MODEL = 1024
CHUNK = 64
HEAD_DIM = 64
EPS = 1e-6
A_HEADS = 8
IDX_HEADS = 8
IDX_DIM = 64
TOPK_MAX = 256
B_VDIM = 128
B_HEADS = 4
C_HEADS = 16
C_BAND = 9
REL_CLIP = 256
T5_BUCKETS = 32
T5_MAX_DIST = 128
M_HEADS = 4
M_DIM = 128
D_FF = 2816

F32 = jnp.float32
BF16 = jnp.bfloat16
I32 = jnp.int32

NEG = -1e30
LOG2E = math.log2(math.e)
INT_MIN = -(2 ** 31)
MIN_NORMAL_BITS = 0x00800000
NEAR_SPAN = 3 * 128
VMEM_LIMIT = 60 * 1024 * 1024

ROW_TILE = 512
FF_CHUNK = 256
TQ = 256
TK = 256
KC = 4
ROWSUM_ROWS = 16
BAND_T = 512


def _cparams(n_axes):
    return pltpu.CompilerParams(dimension_semantics=("arbitrary",) * n_axes,
                                vmem_limit_bytes=VMEM_LIMIT)


def _dot(a, b):
    return jnp.dot(a, b, preferred_element_type=F32)


def _rms_rows(x, g):
    ms = jnp.mean(x * x, axis=-1, keepdims=True)
    return x * lax.rsqrt(ms + EPS) * g


def _ffn_body(h_ref, g_ref, wg_ref, wu_ref, wd_ref, o_ref, xn_ref, acc_ref):
    x = h_ref[...]
    xn_ref[...] = _rms_rows(x, g_ref[...]).astype(BF16)
    acc_ref[...] = jnp.zeros_like(acc_ref)

    def chunk(c, carry):
        xn = xn_ref[...]
        gate = _dot(xn, wg_ref[c])
        up = _dot(xn, wu_ref[c])
        act = (gate * jax.nn.sigmoid(gate) * up).astype(BF16)
        acc_ref[...] += _dot(act, wd_ref[c])
        return carry

    lax.fori_loop(0, wg_ref.shape[0], chunk, 0)
    o_ref[...] = x + 0.5 * acc_ref[...]


def _ffn(h, g, wg, wu, wd):
    s, d = h.shape
    ff = wg.shape[1]
    nc = ff // FF_CHUNK
    wg3 = wg.astype(BF16).reshape(d, nc, FF_CHUNK).transpose(1, 0, 2)
    wu3 = wu.astype(BF16).reshape(d, nc, FF_CHUNK).transpose(1, 0, 2)
    wd3 = wd.astype(BF16).reshape(nc, FF_CHUNK, d)
    tm = min(ROW_TILE, s)
    const3 = lambda i: (0, 0, 0)
    return pl.pallas_call(
        _ffn_body,
        grid=(s // tm,),
        in_specs=[
            pl.BlockSpec((tm, d), lambda i: (i, 0)),
            pl.BlockSpec((1, d), lambda i: (0, 0)),
            pl.BlockSpec((nc, d, FF_CHUNK), const3, pipeline_mode=pl.Buffered(1)),
            pl.BlockSpec((nc, d, FF_CHUNK), const3, pipeline_mode=pl.Buffered(1)),
            pl.BlockSpec((nc, FF_CHUNK, d), const3, pipeline_mode=pl.Buffered(1)),
        ],
        out_specs=pl.BlockSpec((tm, d), lambda i: (i, 0)),
        out_shape=jax.ShapeDtypeStruct((s, d), F32),
        scratch_shapes=[pltpu.VMEM((tm, d), BF16), pltpu.VMEM((tm, d), F32)],
        compiler_params=_cparams(1),
        name="ffn",
    )(h, g.reshape(1, d), wg3, wu3, wd3)


class _Seg:
    def __init__(self, w, head_dim=None, gain=None, scale=1.0, layout="fm",
                 dtype=BF16, tile=TK, pad=0):
        self.w = w
        self.head_dim = head_dim
        self.gain = gain
        self.scale = scale
        self.layout = layout
        self.dtype = dtype
        self.tile = tile
        self.pad = pad


def _proj_body(segs, x_ref, g_ref, *refs):
    n = len(segs)
    w_refs = refs[:n]
    gains = [r for r in refs[n:2 * n]]
    outs = refs[2 * n:]
    xn = _rms_rows(x_ref[...], g_ref[...]).astype(BF16)
    tm = xn.shape[0]
    for seg, w_ref, gain_ref, o_ref in zip(segs, w_refs, gains, outs):
        yt = lax.dot_general(w_ref[...], xn, (((1,), (1,)), ((), ())),
                             preferred_element_type=F32)
        f = yt.shape[0]
        if seg.head_dim is not None:
            hd = seg.head_dim
            y3 = yt.reshape(f // hd, hd, tm)
            ms = jnp.mean(y3 * y3, axis=1, keepdims=True)
            y3 = y3 * lax.rsqrt(ms + EPS) * gain_ref[...][None]
            yt = y3.reshape(f, tm)
        if seg.scale != 1.0:
            yt = yt * seg.scale
        if seg.pad:
            yt = jnp.concatenate([yt, jnp.zeros((seg.pad, tm), F32)], axis=0)
        if seg.layout == "fm":
            o_ref[...] = yt.astype(seg.dtype)
        elif seg.layout == "tm":
            o_ref[...] = yt.T.astype(seg.dtype)
        else:
            for t in range(tm // seg.tile):
                o_ref[t] = yt[:, t * seg.tile:(t + 1) * seg.tile].astype(seg.dtype)


def _project(x, g, segs):
    s, d = x.shape
    tm = min(ROW_TILE, s)
    in_specs = [pl.BlockSpec((tm, d), lambda i: (i, 0)),
                pl.BlockSpec((1, d), lambda i: (0, 0))]
    args = [x, g.reshape(1, d)]
    for seg in segs:
        wt = seg.w.T.astype(BF16)
        args.append(wt)
        in_specs.append(pl.BlockSpec(wt.shape, lambda i: (0, 0)))
    for seg in segs:
        hd = seg.head_dim or 8
        gain = seg.gain if seg.gain is not None else jnp.ones((hd,), F32)
        args.append(gain.astype(F32).reshape(hd, 1))
        in_specs.append(pl.BlockSpec((hd, 1), lambda i: (0, 0)))
    out_shapes, out_specs = [], []
    for seg in segs:
        f = seg.w.shape[1] + seg.pad
        if seg.layout == "fm":
            out_shapes.append(jax.ShapeDtypeStruct((f, s), seg.dtype))
            out_specs.append(pl.BlockSpec((f, tm), lambda i: (0, i)))
        elif seg.layout == "tm":
            out_shapes.append(jax.ShapeDtypeStruct((s, f), seg.dtype))
            out_specs.append(pl.BlockSpec((tm, f), lambda i: (i, 0)))
        else:
            nt = tm // seg.tile
            out_shapes.append(jax.ShapeDtypeStruct((s // seg.tile, f, seg.tile), seg.dtype))
            out_specs.append(pl.BlockSpec((nt, f, seg.tile), lambda i: (i, 0, 0)))
    return pl.pallas_call(
        functools.partial(_proj_body, segs),
        grid=(s // tm,),
        in_specs=in_specs,
        out_specs=out_specs,
        out_shape=out_shapes,
        compiler_params=_cparams(1),
        name="project",
    )(*args)


def _mix_out_mem_body(n, h_ref, *refs):
    ot_refs, w_refs = refs[:n], refs[n:2 * n]
    g_ref, wq_ref, qg_ref, k_ref, vt_ref, wo_ref, o_ref = refs[2 * n:]
    h1 = h_ref[...]
    for ot_ref, w_ref in zip(ot_refs, w_refs):
        h1 = h1 + _dot(ot_ref[...].T.astype(BF16), w_ref[...])
    tm = h1.shape[0]

    xn = _rms_rows(h1, g_ref[...]).astype(BF16)
    qt = lax.dot_general(wq_ref[...], xn, (((1,), (1,)), ((), ())), preferred_element_type=F32)
    q3 = qt.reshape(M_HEADS, M_DIM, tm)
    ms = jnp.mean(q3 * q3, axis=1, keepdims=True)
    q3 = q3 * lax.rsqrt(ms + EPS) * qg_ref[...][None] * (M_DIM ** -0.5 * LOG2E)
    q = q3.reshape(M_HEADS * M_DIM, tm).astype(BF16)

    outs = []
    for hd in range(M_HEADS):
        rows = slice(hd * M_DIM, (hd + 1) * M_DIM)
        s = _dot(k_ref[:, rows], q[rows, :])
        p = jnp.exp2(s - jnp.max(s, axis=0, keepdims=True))
        outs.append(_normalised(_dot(_with_ones(vt_ref[0, rows, :]), p.astype(BF16)), M_DIM))
    o = jnp.concatenate(outs, axis=0)
    o_ref[...] = h1 + _dot(o.T.astype(BF16), wo_ref[...])


def _mix_out_mem(h, ots, ws, g, wq, qg, k, v_t, wo):
    s, d = h.shape
    tm = min(ROW_TILE, s)
    n = len(ots)
    const2 = lambda i: (0, 0)
    in_specs = [pl.BlockSpec((tm, d), lambda i: (i, 0))]
    in_specs += [pl.BlockSpec((ot.shape[0], tm), lambda i: (0, i)) for ot in ots]
    in_specs += [pl.BlockSpec(w.shape, const2) for w in ws]
    in_specs += [
        pl.BlockSpec((1, d), const2),
        pl.BlockSpec((wq.shape[1], d), const2),
        pl.BlockSpec((M_DIM, 1), const2),
        pl.BlockSpec(k.shape, const2),
        pl.BlockSpec(v_t.shape, lambda i: (0, 0, 0)),
        pl.BlockSpec(wo.shape, const2),
    ]
    return pl.pallas_call(
        functools.partial(_mix_out_mem_body, n),
        grid=(s // tm,),
        in_specs=in_specs,
        out_specs=pl.BlockSpec((tm, d), lambda i: (i, 0)),
        out_shape=jax.ShapeDtypeStruct((s, d), F32),
        compiler_params=_cparams(1),
        name="mix_out_mem",
    )(h, *ots, *[w.astype(BF16) for w in ws], g.reshape(1, d), wq.T.astype(BF16),
      qg.astype(F32).reshape(M_DIM, 1), k, v_t, wo.astype(BF16))


def _softmax_step_ref(s_ref, stream, vts, m_ref, acc_ref):
    m_old = m_ref[stream]
    m_new = jnp.maximum(m_old, jnp.max(s_ref[stream], axis=0, keepdims=True))
    pv = None
    for u in range(KC):
        p = jnp.exp2(s_ref[stream, u * TK:(u + 1) * TK, :] - m_new)
        d = _dot(vts[u], p.astype(BF16))
        pv = d if pv is None else pv + d
    acc_ref[stream] = jnp.exp2(m_old - m_new) * acc_ref[stream] + pv
    m_ref[stream] = m_new


def _with_ones(vt):
    return jnp.concatenate([vt, jnp.ones((ROWSUM_ROWS, vt.shape[1]), vt.dtype)], axis=0)


def _normalised(acc, dv):
    return acc[:dv] / acc[dv:dv + 1]


def _score_pipeline(i, n_chunks, k_ref, q_halves, bias_ref, heads, mask_fn, consume,
                    s_even, s_odd):
    def produce(near, c, s_ref):
        for u in range(KC):
            j = c * KC + u
            k0 = pl.multiple_of(j * TK, TK)
            kt = k_ref[pl.ds(k0, TK), :]
            dist = i - j
            which = jnp.where(dist < 0, 3, jnp.minimum(dist, 2))
            sel = None if mask_fn is None else mask_fn(k0)
            for a, qh in enumerate(q_halves):
                s = _dot(kt, qh)
                if near:
                    s = s + bias_ref[heads[a], which]
                s_ref[a, u * TK:(u + 1) * TK, :] = s if sel is None else jnp.where(sel, s, NEG)

    def pair(near, p, carry):
        c = 2 * p
        produce(near, c + 1, s_odd)
        consume(c, s_even)
        produce(near, c + 2, s_even)
        consume(c + 1, s_odd)
        return carry

    n_far = jnp.maximum(i - 1, 0) // KC
    far_pairs = jnp.maximum(n_far - 1, 0) // 2
    full_pairs = (n_chunks - 1) // 2
    produce(True, 0, s_even)
    lax.fori_loop(0, far_pairs, functools.partial(pair, False), 0)
    lax.fori_loop(far_pairs, full_pairs, functools.partial(pair, True), 0)

    @pl.when(n_chunks % 2 == 1)
    def _last_one():
        consume(n_chunks - 1, s_even)

    @pl.when(n_chunks % 2 == 0)
    def _last_two():
        produce(True, n_chunks - 1, s_odd)
        consume(n_chunks - 2, s_even)
        consume(n_chunks - 1, s_odd)


def _init_state(m_ref, acc_ref):
    m_ref[...] = jnp.full(m_ref.shape, NEG, F32)
    acc_ref[...] = jnp.zeros_like(acc_ref)


def _half_masked(qpair):
    row = lax.broadcasted_iota(I32, qpair.shape, 0)
    zero = jnp.zeros_like(qpair)
    return jnp.where(row < HEAD_DIM, qpair, zero), jnp.where(row >= HEAD_DIM, qpair, zero)


def _t5_bucket(rel):
    nb = T5_BUCKETS // 2
    max_exact = nb // 2
    offset = (rel < 0).astype(jnp.int32) * nb
    n = jnp.abs(rel)
    nf = jnp.maximum(n, 1).astype(jnp.float32)
    large = max_exact + (jnp.log(nf / max_exact) / math.log(T5_MAX_DIST / max_exact)
                         * (nb - max_exact)).astype(jnp.int32)
    large = jnp.minimum(large, nb - 1)
    return offset + jnp.where(n < max_exact, n, large)


def _toeplitz_body(masks, fills, n_keys, n_q, w_ref, o_ref):
    sk = lax.broadcasted_iota(I32, (n_keys, n_q), 0)
    tq = lax.broadcasted_iota(I32, (n_keys, n_q), 1)
    for blk, mask in enumerate(masks):
        w = jnp.broadcast_to(w_ref[0, blk], (n_keys, w_ref.shape[-1]))
        t = pltpu.roll(w, 0, 1, stride=1, stride_axis=0)[:, :n_q]
        o_ref[0, blk] = t if mask is None else jnp.where(mask(sk, tq), t, NEG)
    for n, fill in enumerate(fills):
        o_ref[0, len(masks) + n] = jnp.full((n_keys, n_q), fill, F32)


def _toeplitz_tiles(u, n_keys, n_q, masks, fills=()):
    hn, nb, _ = u.shape
    width = pl.cdiv(n_keys + n_q - 1, 128) * 128
    w = jnp.concatenate([u[..., n_keys - 1:],
                         jnp.zeros((hn, nb, width - (n_keys + n_q - 1)), F32),
                         u[..., :n_keys - 1]], axis=-1)
    n_out = nb + len(fills)
    return pl.pallas_call(
        functools.partial(_toeplitz_body, tuple(masks), tuple(fills), n_keys, n_q),
        grid=(hn,),
        in_specs=[pl.BlockSpec((1, nb, 1, width), lambda h: (h, 0, 0, 0))],
        out_specs=pl.BlockSpec((1, n_out, n_keys, n_q), lambda h: (h, 0, 0, 0)),
        out_shape=jax.ShapeDtypeStruct((hn, n_out, n_keys, n_q), F32),
        compiler_params=_cparams(1),
        name="toeplitz_tiles",
    )(w.reshape(hn, nb, 1, width))


def _t5_bias_tiles(table):
    table = table.astype(F32)
    far = table[_t5_bucket(jnp.full((1,), 2 * TK, jnp.int32))].T
    r = jnp.arange(-(TK - 1), TQ, dtype=jnp.int32)
    u = jnp.stack([(table[_t5_bucket(dist * TK + r)].T - far) * LOG2E for dist in range(2)],
                  axis=1)
    chunk_causal = lambda sk, tq: (sk // CHUNK) <= (tq // CHUNK)
    return _toeplitz_tiles(u, TK, TQ, (chunk_causal, None), fills=(0.0, NEG))


def _key_to_f32(key):
    return lax.bitcast_convert_type(jnp.where(key < 0, INT_MIN - key, key), F32)


def _score_codes(x):
    b = lax.bitcast_convert_type(x, I32)
    b = jnp.where((b & 0x7FFFFFFF) < MIN_NORMAL_BITS, 0, b)
    key = jnp.where(b < 0, INT_MIN - b, b)
    fb = jnp.where(b < 0, b + 0xFFFF, b) & jnp.int32(-65536)
    return key, lax.bitcast_convert_type(fb, F32).astype(BF16)


def _tree_sum(x):
    while x.shape[0] > 1:
        half = x.shape[0] // 2
        x = x[:half] + x[half:]
    return x[0]


def _select_topk(keys_ref, hb_ref, thr_ref, max_key, n_chunks, top_k, idx_bits):
    rows = KC * TK
    one, zero = jnp.ones((), BF16), jnp.zeros((), BF16)

    def count16(cand):
        cand = jnp.where((cand > 0) & (cand < MIN_NORMAL_BITS), MIN_NORMAL_BITS, cand)
        tb = _key_to_f32(cand).astype(BF16)

        def body(c, acc):
            c0 = pl.multiple_of(c * rows, rows)
            ge = jnp.where(hb_ref[pl.ds(c0, rows), :] >= tb, one, zero)
            return acc + _tree_sum(ge.reshape(rows // 16, 16, TQ)).astype(F32)

        acc = lax.fori_loop(0, n_chunks, body, jnp.zeros((16, TQ), F32))
        return jnp.sum(acc, axis=0, keepdims=True).astype(I32)

    def count32(pred):
        def body(c, acc):
            c0 = pl.multiple_of(c * rows, rows)
            hit = pred(keys_ref[pl.ds(c0, rows), :], c0).astype(I32)
            return acc + jnp.sum(hit.reshape(rows // 8, 8, TQ), axis=0)

        acc = lax.fori_loop(0, n_chunks, body, jnp.zeros((8, TQ), I32))
        return jnp.sum(acc, axis=0, keepdims=True)

    def search(count, bits, thr, c_thr, n_fixed):
        def step(carry):
            b, thr, c_thr = carry
            cand = thr + lax.shift_left(jnp.int32(1), bits[0] - b)
            cnt = count(cand)
            ok = cnt >= top_k
            return b + 1, jnp.where(ok, cand, thr), jnp.where(ok, cnt, c_thr)

        def unresolved(carry):
            b, _, c_thr = carry
            return (b <= bits[0] - bits[1]) & (jnp.max(c_thr) > top_k)

        carry = lax.fori_loop(0, n_fixed, lambda _, c: step(c), (jnp.int32(0), thr, c_thr))
        return lax.while_loop(unresolved, step, carry)[1:]

    hi = lax.shift_right_arithmetic(max_key, 16) + 1
    lo = jnp.maximum(hi - NEAR_SPAN, INT_MIN >> 16)
    cnt = count16(lax.shift_left(lo, 16))
    ok = cnt >= top_k
    hi = jnp.where(ok, hi, lo)
    lo = jnp.where(ok, lo, INT_MIN >> 16)
    c_lo = jnp.where(ok, cnt, 2 ** 30)

    def halve(_, carry):
        lo, hi, c_lo = carry
        mid = lax.shift_right_arithmetic(lo + hi, 1)
        cnt = count16(lax.shift_left(mid, 16))
        up = (hi - lo > 1) & (cnt >= top_k)
        down = (hi - lo > 1) & (cnt < top_k)
        return jnp.where(up, mid, lo), jnp.where(down, mid, hi), jnp.where(up, cnt, c_lo)

    n_halvings = jnp.where(jnp.min(ok.astype(I32)) > 0, NEAR_SPAN.bit_length(), 16)
    lo, _, c_thr = lax.fori_loop(0, n_halvings, halve, (lo, hi, c_lo))
    thr = lax.shift_left(lo, 16)
    thr, c_thr = search(lambda cand: count32(lambda keys, c0: keys >= cand), (15, 0), thr, c_thr, 8)
    thr_ref[...] = jnp.maximum(thr, INT_MIN + 1)

    split = (c_thr > top_k) & (thr > INT_MIN)

    @pl.when(jnp.max(split.astype(I32)) > 0)
    def _break_ties():
        need = top_k - count32(lambda keys, c0: keys > thr)

        def equal_before(limit):
            def pred(keys, c0):
                idx = c0 + lax.broadcasted_iota(I32, keys.shape, 0)
                return (keys == thr) & (idx < limit)
            return count32(pred)

        def step(b, q):
            cand = q + lax.shift_left(jnp.int32(1), idx_bits - 1 - b)
            return jnp.where(equal_before(cand) < need, cand, q)

        q = lax.fori_loop(0, idx_bits, step, jnp.zeros((1, TQ), I32))

        def lower(c, carry):
            c0 = pl.multiple_of(c * rows, rows)
            keys = keys_ref[pl.ds(c0, rows), :]
            idx = c0 + lax.broadcasted_iota(I32, keys.shape, 0)
            drop = split & (keys == thr) & (idx > q)
            keys_ref[pl.ds(c0, rows), :] = jnp.where(drop, thr - 1, keys)
            return carry

        lax.fori_loop(0, n_chunks, lower, 0)


def _dsa_body(top_k, idx_bits, iq_ref, iw_ref, ik_ref, q_ref, k_ref, vt_ref, bias_ref, o_ref,
              keys_ref, hb_ref, thr_ref, m_ref, acc_ref, s_even, s_odd):
    i = pl.program_id(0)
    g = pl.program_id(1)
    n_tiles = i + 1
    n_chunks = (i + KC) // KC

    @pl.when(g == 0)
    def _select():
        w_all = iw_ref[...]
        zpad = jnp.zeros((IDX_DIM, TQ), BF16)
        qz = [jnp.concatenate([iq_ref[h * IDX_DIM:(h + 1) * IDX_DIM, :], zpad], axis=0)
              for h in range(IDX_HEADS)]

        def score_tiles(p, top):
            for u in range(2):
                k0 = pl.multiple_of((2 * p + u) * TK, TK)
                ikt = ik_ref[pl.ds(k0, TK), :]
                sc = jnp.zeros((TK, TQ), F32)
                for h in range(IDX_HEADS):
                    sc = sc + w_all[h:h + 1, :] * jnp.maximum(_dot(ikt, qz[h]), 0.0)
                keys_ref[pl.ds(k0, TK), :], hb_ref[pl.ds(k0, TK), :] = _score_codes(sc)
                top = jnp.maximum(top, jnp.max(sc.reshape(TK // 8, 8, TQ), axis=0))
            return top

        top = lax.fori_loop(0, (n_tiles + 1) // 2, score_tiles, jnp.full((8, TQ), -jnp.inf, F32))
        max_key = _score_codes(jnp.max(top, axis=0, keepdims=True))[0]

        d0 = pl.multiple_of(i * TK, TK)
        sk = lax.broadcasted_iota(I32, (TK, TQ), 0) // CHUNK
        tq = lax.broadcasted_iota(I32, (TK, TQ), 1) // CHUNK
        no_key = jnp.full((TK, TQ), INT_MIN, I32)
        no_hb = jnp.full((TK, TQ), jnp.nan, BF16)
        keys_ref[pl.ds(d0, TK), :] = jnp.where(sk <= tq, keys_ref[pl.ds(d0, TK), :], no_key)
        hb_ref[pl.ds(d0, TK), :] = jnp.where(sk <= tq, hb_ref[pl.ds(d0, TK), :], no_hb)

        def fill_tile(j, carry):
            k0 = pl.multiple_of(j * TK, TK)
            keys_ref[pl.ds(k0, TK), :] = no_key
            hb_ref[pl.ds(k0, TK), :] = no_hb
            return carry

        lax.fori_loop(n_tiles, n_chunks * KC, fill_tile, 0)
        _select_topk(keys_ref, hb_ref, thr_ref, max_key, n_chunks, top_k, idx_bits)

    _init_state(m_ref, acc_ref)
    q_halves = _half_masked(q_ref[...])
    thr = thr_ref[...]

    def selected(k0):
        return keys_ref[pl.ds(k0, TK), :] >= thr

    def consume(c, s_ref):
        for a in range(2):
            vts = [_with_ones(vt_ref[c * KC + u, a * HEAD_DIM:(a + 1) * HEAD_DIM, :])
                   for u in range(KC)]
            _softmax_step_ref(s_ref, a, vts, m_ref, acc_ref)

    _score_pipeline(i, n_chunks, k_ref, q_halves, bias_ref, (0, 1), selected, consume,
                    s_even, s_odd)
    for a in range(2):
        o_ref[a * HEAD_DIM:(a + 1) * HEAD_DIM, :] = _normalised(acc_ref[a], HEAD_DIM)


def _dsa_attention(iq_t, iw_t, ik, q_t, k, v_t, bias, top_k):
    s = k.shape[0]
    n_pairs = A_HEADS // 2
    return pl.pallas_call(
        functools.partial(_dsa_body, top_k, (s - 1).bit_length()),
        grid=(s // TQ, n_pairs),
        in_specs=[
            pl.BlockSpec((IDX_HEADS * IDX_DIM, TQ), lambda i, g: (0, i)),
            pl.BlockSpec((IDX_HEADS, TQ), lambda i, g: (0, i)),
            pl.BlockSpec((s, 128), lambda i, g: (0, 0)),
            pl.BlockSpec((128, TQ), lambda i, g: (g, i)),
            pl.BlockSpec((s, 128), lambda i, g: (0, g)),
            pl.BlockSpec((s // TK, 128, TK), lambda i, g: (0, g, 0)),
            pl.BlockSpec((2, 4, TK, TQ), lambda i, g: (g, 0, 0, 0)),
        ],
        out_specs=pl.BlockSpec((128, TQ), lambda i, g: (g, i)),
        out_shape=jax.ShapeDtypeStruct((A_HEADS * HEAD_DIM, s), F32),
        scratch_shapes=[
            pltpu.VMEM((s, TQ), I32),
            pltpu.VMEM((s, TQ), BF16),
            pltpu.VMEM((1, TQ), I32),
            pltpu.VMEM((2, 1, TQ), F32),
            pltpu.VMEM((2, HEAD_DIM + ROWSUM_ROWS, TQ), F32),
            pltpu.VMEM((2, KC * TK, TQ), F32),
            pltpu.VMEM((2, KC * TK, TQ), F32),
        ],
        compiler_params=_cparams(2),
        name="dsa_attention",
    )(iq_t, iw_t, ik, q_t, k, v_t, bias)


def _diff_body(lambda_init, q_ref, k_ref, vt_ref, bias_ref, lam_ref, subln_ref, o_ref,
               m_ref, acc_ref, s_even, s_odd):
    i = pl.program_id(0)
    _init_state(m_ref, acc_ref)
    q_halves = _half_masked(q_ref[...])

    def consume(c, s_ref):
        vts = [_with_ones(vt_ref[c * KC + u]) for u in range(KC)]
        for a in range(2):
            _softmax_step_ref(s_ref, a, vts, m_ref, acc_ref)

    _score_pipeline(i, (i + KC) // KC, k_ref, q_halves, bias_ref, (0, 0), None, consume,
                    s_even, s_odd)

    lq1, lk1, lq2, lk2 = (lam_ref[r:r + 1, :] for r in range(4))
    lam = (jnp.exp(jnp.sum(lq1 * lk1, keepdims=True)) - jnp.exp(jnp.sum(lq2 * lk2, keepdims=True))
           + lambda_init)
    o = _normalised(acc_ref[0], B_VDIM) - lam * _normalised(acc_ref[1], B_VDIM)
    ms = jnp.mean(o * o, axis=0, keepdims=True)
    o_ref[...] = o * lax.rsqrt(ms + EPS) * subln_ref[...] * (1.0 - lambda_init)


def _diff_attention(q_t, k, v_t, bias, lam_rows, subln, lambda_init):
    s = k.shape[0]
    return pl.pallas_call(
        functools.partial(_diff_body, lambda_init),
        grid=(s // TQ, B_HEADS),
        in_specs=[
            pl.BlockSpec((128, TQ), lambda i, h: (h, i)),
            pl.BlockSpec((s, 128), lambda i, h: (0, h)),
            pl.BlockSpec((s // TK, B_VDIM, TK), lambda i, h: (0, h, 0)),
            pl.BlockSpec((1, 4, TK, TQ), lambda i, h: (h, 0, 0, 0)),
            pl.BlockSpec((4, HEAD_DIM), lambda i, h: (0, 0)),
            pl.BlockSpec((B_VDIM, 1), lambda i, h: (0, 0)),
        ],
        out_specs=pl.BlockSpec((B_VDIM, TQ), lambda i, h: (h, i)),
        out_shape=jax.ShapeDtypeStruct((B_HEADS * B_VDIM, s), F32),
        scratch_shapes=[
            pltpu.VMEM((2, 1, TQ), F32),
            pltpu.VMEM((2, B_VDIM + ROWSUM_ROWS, TQ), F32),
            pltpu.VMEM((2, KC * TK, TQ), F32),
            pltpu.VMEM((2, KC * TK, TQ), F32),
        ],
        compiler_params=_cparams(2),
        name="diff_attention",
    )(q_t, k, v_t, bias, lam_rows, subln.astype(F32).reshape(B_VDIM, 1))


def _band_bias_tiles(rel_bias):
    table = rel_bias.astype(F32) * LOG2E
    r = jnp.arange(-(BAND_T - 1), BAND_T, dtype=jnp.int32)
    u = jnp.stack([table[jnp.clip((1 - blk) * BAND_T + r, -REL_CLIP, REL_CLIP) + REL_CLIP].T
                   for blk in range(2)], axis=1)

    def in_band(blk):
        def mask(sk, tq):
            kc = sk // CHUNK + blk * (BAND_T // CHUNK)
            qc = tq // CHUNK + BAND_T // CHUNK
            return (kc <= qc) & (kc >= qc - (C_BAND - 1))
        return mask

    return _toeplitz_tiles(u, BAND_T, BAND_T, (in_band(0), in_band(1)))


def _band_body(q_ref, kp_ref, kc_ref, vp_ref, vc_ref, bias_ref, o_ref):
    i = pl.program_id(1)
    q_halves = _half_masked(q_ref[...])
    for a in range(2):
        rows = slice(a * HEAD_DIM, (a + 1) * HEAD_DIM)
        s_prev = _dot(kp_ref[...], q_halves[a]) + jnp.where(i > 0, bias_ref[a, 0], NEG)
        s_cur = _dot(kc_ref[...], q_halves[a]) + bias_ref[a, 1]
        m = jnp.maximum(jnp.max(s_prev, axis=0, keepdims=True),
                        jnp.max(s_cur, axis=0, keepdims=True))
        acc = (_dot(_with_ones(vp_ref[0, rows, :]), jnp.exp2(s_prev - m).astype(BF16))
               + _dot(_with_ones(vc_ref[0, rows, :]), jnp.exp2(s_cur - m).astype(BF16)))
        o_ref[rows, :] = _normalised(acc, HEAD_DIM)


def _band_attention(q_t, k, v_t, bias):
    s = k.shape[0]
    t = BAND_T
    prev = lambda i: jnp.maximum(i - 1, 0)
    return pl.pallas_call(
        _band_body,
        grid=(C_HEADS // 2, s // t),
        in_specs=[
            pl.BlockSpec((128, t), lambda p, i: (p, i)),
            pl.BlockSpec((t, 128), lambda p, i: (prev(i), p)),
            pl.BlockSpec((t, 128), lambda p, i: (i, p)),
            pl.BlockSpec((1, 128, t), lambda p, i: (prev(i), p, 0)),
            pl.BlockSpec((1, 128, t), lambda p, i: (i, p, 0)),
            pl.BlockSpec((2, 2, t, t), lambda p, i: (p, 0, 0, 0)),
        ],
        out_specs=pl.BlockSpec((128, t), lambda p, i: (p, i)),
        out_shape=jax.ShapeDtypeStruct((C_HEADS * HEAD_DIM, s), F32),
        compiler_params=_cparams(2),
        name="band_attention",
    )(q_t, k, k, v_t, v_t, bias)


def _even_mixer(h, mix_g, w_in, a_qg, a_kg, idx_kg, b_qg, b_kg, lq1, lk1, lq2, lk2, b_subln,
                w_out, t5_bias, layer_idx):
    s = h.shape[0]
    sizes = [A_HEADS * HEAD_DIM] * 3 + [IDX_HEADS * IDX_DIM, IDX_DIM, IDX_HEADS] + \
            [B_HEADS * 2 * HEAD_DIM] * 2 + [B_HEADS * B_VDIM]
    cuts = np.cumsum([0] + sizes)
    w = [w_in[:, cuts[n]:cuts[n + 1]] for n in range(len(sizes))]
    qk_scale = HEAD_DIM ** -0.5 * LOG2E
    segs = [
        _Seg(w[0], HEAD_DIM, a_qg, qk_scale, "fm"),
        _Seg(w[1], HEAD_DIM, a_kg, 1.0, "tm"),
        _Seg(w[2], layout="vt"),
        _Seg(w[3], scale=IDX_DIM ** -0.5, layout="fm"),
        _Seg(w[4], IDX_DIM, idx_kg, 1.0, "tm", pad=IDX_DIM),
        _Seg(w[5], scale=IDX_HEADS ** -0.5, layout="fm", dtype=F32),
        _Seg(w[6], HEAD_DIM, b_qg, qk_scale, "fm"),
        _Seg(w[7], HEAD_DIM, b_kg, 1.0, "tm"),
        _Seg(w[8], layout="vt"),
    ]
    aq_t, ak, av_t, iq_t, ik, iw_t, bq_t, bk, bv_t = _project(h, mix_g, segs)
    bias = _t5_bias_tiles(t5_bias)
    assert s % (KC * TK) == 0
    top_k = min(TOPK_MAX, s // 4)
    out_a = _dsa_attention(iq_t, iw_t, ik, aq_t, ak, av_t, bias[:A_HEADS], top_k)
    lambda_init = 0.8 - 0.6 * math.exp(-0.3 * layer_idx)
    lam_rows = jnp.stack([lq1, lk1, lq2, lk2]).astype(F32)
    out_b = _diff_attention(bq_t, bk, bv_t, bias[A_HEADS:], lam_rows, b_subln, lambda_init)
    na = A_HEADS * HEAD_DIM
    return [out_a, out_b], [w_out[:na], w_out[na:]]


def _odd_mixer(h, mix_g, w_in, c_qg, c_kg, rel_bias, w_out):
    f = C_HEADS * HEAD_DIM
    segs = [
        _Seg(w_in[:, :f], HEAD_DIM, c_qg, HEAD_DIM ** -0.5 * LOG2E, "fm"),
        _Seg(w_in[:, f:2 * f], HEAD_DIM, c_kg, 1.0, "tm"),
        _Seg(w_in[:, 2 * f:], layout="vt", tile=BAND_T),
    ]
    q_t, k, v_t = _project(h, mix_g, segs)
    out = _band_attention(q_t, k, v_t, _band_bias_tiles(rel_bias))
    return [out], [w_out]


def _mixer_out_and_memory(h, mixer_out, mem, mg, sg, wq, wkv, qg, kg, wo):
    ots, ws = mixer_out
    f = M_HEADS * M_DIM
    k, v_t = _project(mem, sg, [_Seg(wkv[:, :f], M_DIM, kg, 1.0, "tm"),
                                _Seg(wkv[:, f:], layout="vt", tile=mem.shape[0])])
    return _mix_out_mem(h, ots, ws, mg, wq, qg, k, v_t, wo)


def kernel(x, mem, t5_bias,
           l0_ffn1_norm, l0_ffn1_wg, l0_ffn1_wu, l0_ffn1_wd,
           l0_mix_norm, l0_w_in, l0_a_q_norm, l0_a_k_norm, l0_idx_k_norm,
           l0_b_q_norm, l0_b_k_norm, l0_b_lq1, l0_b_lk1, l0_b_lq2, l0_b_lk2, l0_b_subln, l0_w_out,
           l0_mem_norm, l0_mem_src_norm, l0_mem_wq, l0_mem_wkv, l0_mem_q_norm, l0_mem_k_norm, l0_mem_wo,
           l0_ffn2_norm, l0_ffn2_wg, l0_ffn2_wu, l0_ffn2_wd,
           l1_ffn1_norm, l1_ffn1_wg, l1_ffn1_wu, l1_ffn1_wd,
           l1_mix_norm, l1_w_in, l1_c_q_norm, l1_c_k_norm, l1_c_rel_bias, l1_w_out,
           l1_mem_norm, l1_mem_src_norm, l1_mem_wq, l1_mem_wkv, l1_mem_q_norm, l1_mem_k_norm, l1_mem_wo,
           l1_ffn2_norm, l1_ffn2_wg, l1_ffn2_wu, l1_ffn2_wd):
    bsz, seq, d = x.shape
    assert bsz == 1 and mem.shape[0] == 1
    h = x.reshape(seq, d)
    m = mem.reshape(mem.shape[1], d)

    h = _ffn(h, l0_ffn1_norm, l0_ffn1_wg, l0_ffn1_wu, l0_ffn1_wd)
    mixed = _even_mixer(h, l0_mix_norm, l0_w_in, l0_a_q_norm, l0_a_k_norm, l0_idx_k_norm,
                        l0_b_q_norm, l0_b_k_norm, l0_b_lq1, l0_b_lk1, l0_b_lq2, l0_b_lk2,
                        l0_b_subln, l0_w_out, t5_bias, 0)
    h = _mixer_out_and_memory(h, mixed, m, l0_mem_norm, l0_mem_src_norm, l0_mem_wq, l0_mem_wkv,
                              l0_mem_q_norm, l0_mem_k_norm, l0_mem_wo)
    h = _ffn(h, l0_ffn2_norm, l0_ffn2_wg, l0_ffn2_wu, l0_ffn2_wd)

    h = _ffn(h, l1_ffn1_norm, l1_ffn1_wg, l1_ffn1_wu, l1_ffn1_wd)
    mixed = _odd_mixer(h, l1_mix_norm, l1_w_in, l1_c_q_norm, l1_c_k_norm, l1_c_rel_bias, l1_w_out)
    h = _mixer_out_and_memory(h, mixed, m, l1_mem_norm, l1_mem_src_norm, l1_mem_wq, l1_mem_wkv,
                              l1_mem_q_norm, l1_mem_k_norm, l1_mem_wo)
    h = _ffn(h, l1_ffn2_norm, l1_ffn2_wg, l1_ffn2_wu, l1_ffn2_wd)
    return h.reshape(bsz, seq, d)
```

```python
import functools
import math

import jax
import jax.numpy as jnp
import numpy as np
from jax import lax
from jax.experimental import pallas as pl
from jax.experimental.pallas import tpu as pltpu

D_MODEL = 1024
CHUNK = 64
HEAD_DIM = 64
EPS = 1e-6
A_HEADS = 8
IDX_HEADS = 8
IDX_DIM = 64
TOPK_MAX = 256
B_VDIM = 128
B_HEADS = 4
C_HEADS = 16
C_BAND = 9
REL_CLIP = 256
T5_BUCKETS = 32
T5_MAX_DIST = 128
M_HEADS = 4
M_DIM = 128
D_FF = 2816

F32 = jnp.float32
BF16 = jnp.bfloat16
I32 = jnp.int32

NEG = -1e30
LOG2E = math.log2(math.e)
INT_MIN = -(2 ** 31)
MIN_NORMAL_BITS = 0x00800000
VMEM_LIMIT = 60 * 1024 * 1024

ROW_TILE = 512
FF_CHUNK = 256
TQ = 256
TK = 256
KC = 4
ROWSUM_ROWS = 16
BAND_T = 512
BAND_HEADS = 4
DIFF_HEADS = 1


def _cparams(n_axes):
    return pltpu.CompilerParams(dimension_semantics=("arbitrary",) * n_axes,
                                vmem_limit_bytes=VMEM_LIMIT)


def _dot(a, b):
    return jnp.dot(a, b, preferred_element_type=F32)


def _rms_rows(x, g):
    ms = jnp.mean(x * x, axis=-1, keepdims=True)
    return x * lax.rsqrt(ms + EPS) * g


def _ffn_body(h_ref, g_ref, wg_ref, wu_ref, wd_ref, o_ref, xn_ref, acc_ref):
    x = h_ref[...]
    xn_ref[...] = _rms_rows(x, g_ref[...]).astype(BF16)
    acc_ref[...] = jnp.zeros_like(acc_ref)

    def chunk(c, carry):
        xn = xn_ref[...]
        gate = _dot(xn, wg_ref[c])
        up = _dot(xn, wu_ref[c])
        act = (gate * jax.nn.sigmoid(gate) * up).astype(BF16)
        acc_ref[...] += _dot(act, wd_ref[c])
        return carry

    lax.fori_loop(0, wg_ref.shape[0], chunk, 0)
    o_ref[...] = x + 0.5 * acc_ref[...]


def _ffn(h, g, wg, wu, wd):
    s, d = h.shape
    ff = wg.shape[1]
    nc = ff // FF_CHUNK
    wg3 = wg.astype(BF16).reshape(d, nc, FF_CHUNK).transpose(1, 0, 2)
    wu3 = wu.astype(BF16).reshape(d, nc, FF_CHUNK).transpose(1, 0, 2)
    wd3 = wd.astype(BF16).reshape(nc, FF_CHUNK, d)
    tm = min(ROW_TILE, s)
    const3 = lambda i: (0, 0, 0)
    return pl.pallas_call(
        _ffn_body,
        grid=(s // tm,),
        in_specs=[
            pl.BlockSpec((tm, d), lambda i: (i, 0)),
            pl.BlockSpec((1, d), lambda i: (0, 0)),
            pl.BlockSpec((nc, d, FF_CHUNK), const3, pipeline_mode=pl.Buffered(1)),
            pl.BlockSpec((nc, d, FF_CHUNK), const3, pipeline_mode=pl.Buffered(1)),
            pl.BlockSpec((nc, FF_CHUNK, d), const3, pipeline_mode=pl.Buffered(1)),
        ],
        out_specs=pl.BlockSpec((tm, d), lambda i: (i, 0)),
        out_shape=jax.ShapeDtypeStruct((s, d), F32),
        scratch_shapes=[pltpu.VMEM((tm, d), BF16), pltpu.VMEM((tm, d), F32)],
        compiler_params=_cparams(1),
        name="ffn",
    )(h, g.reshape(1, d), wg3, wu3, wd3)


class _Seg:
    def __init__(self, w, head_dim=None, gain=None, scale=1.0, layout="fm",
                 dtype=BF16, tile=TK, pad=0):
        self.w = w
        self.head_dim = head_dim
        self.gain = gain
        self.scale = scale
        self.layout = layout
        self.dtype = dtype
        self.tile = tile
        self.pad = pad


def _proj_body(segs, x_ref, g_ref, *refs):
    n = len(segs)
    w_refs = refs[:n]
    gains = [r for r in refs[n:2 * n]]
    outs = refs[2 * n:]
    xn = _rms_rows(x_ref[...], g_ref[...]).astype(BF16)
    tm = xn.shape[0]
    for seg, w_ref, gain_ref, o_ref in zip(segs, w_refs, gains, outs):
        yt = lax.dot_general(w_ref[...], xn, (((1,), (1,)), ((), ())),
                             preferred_element_type=F32)
        f = yt.shape[0]
        if seg.head_dim is not None:
            hd = seg.head_dim
            y3 = yt.reshape(f // hd, hd, tm)
            ms = jnp.mean(y3 * y3, axis=1, keepdims=True)
            y3 = y3 * lax.rsqrt(ms + EPS) * gain_ref[...][None]
            yt = y3.reshape(f, tm)
        if seg.scale != 1.0:
            yt = yt * seg.scale
        if seg.pad:
            yt = jnp.concatenate([yt, jnp.zeros((seg.pad, tm), F32)], axis=0)
        if seg.layout == "fm":
            o_ref[...] = yt.astype(seg.dtype)
        elif seg.layout == "tm":
            o_ref[...] = yt.T.astype(seg.dtype)
        else:
            for t in range(tm // seg.tile):
                o_ref[t] = yt[:, t * seg.tile:(t + 1) * seg.tile].astype(seg.dtype)


def _project(x, g, segs):
    s, d = x.shape
    tm = min(ROW_TILE, s)
    in_specs = [pl.BlockSpec((tm, d), lambda i: (i, 0)),
                pl.BlockSpec((1, d), lambda i: (0, 0))]
    args = [x, g.reshape(1, d)]
    for seg in segs:
        wt = seg.w.T.astype(BF16)
        args.append(wt)
        in_specs.append(pl.BlockSpec(wt.shape, lambda i: (0, 0)))
    for seg in segs:
        hd = seg.head_dim or 8
        gain = seg.gain if seg.gain is not None else jnp.ones((hd,), F32)
        args.append(gain.astype(F32).reshape(hd, 1))
        in_specs.append(pl.BlockSpec((hd, 1), lambda i: (0, 0)))
    out_shapes, out_specs = [], []
    for seg in segs:
        f = seg.w.shape[1] + seg.pad
        if seg.layout == "fm":
            out_shapes.append(jax.ShapeDtypeStruct((f, s), seg.dtype))
            out_specs.append(pl.BlockSpec((f, tm), lambda i: (0, i)))
        elif seg.layout == "tm":
            out_shapes.append(jax.ShapeDtypeStruct((s, f), seg.dtype))
            out_specs.append(pl.BlockSpec((tm, f), lambda i: (i, 0)))
        else:
            nt = tm // seg.tile
            out_shapes.append(jax.ShapeDtypeStruct((s // seg.tile, f, seg.tile), seg.dtype))
            out_specs.append(pl.BlockSpec((nt, f, seg.tile), lambda i: (i, 0, 0)))
    return pl.pallas_call(
        functools.partial(_proj_body, segs),
        grid=(s // tm,),
        in_specs=in_specs,
        out_specs=out_specs,
        out_shape=out_shapes,
        compiler_params=_cparams(1),
        name="project",
    )(*args)


def _mix_out_mem_body(n, h_ref, *refs):
    ot_refs, w_refs = refs[:n], refs[n:2 * n]
    g_ref, wq_ref, qg_ref, k_ref, vt_ref, wo_ref, o_ref = refs[2 * n:]
    h1 = h_ref[...]
    for ot_ref, w_ref in zip(ot_refs, w_refs):
        h1 = h1 + _dot(ot_ref[...].T.astype(BF16), w_ref[...])
    tm = h1.shape[0]

    xn = _rms_rows(h1, g_ref[...]).astype(BF16)
    qt = lax.dot_general(wq_ref[...], xn, (((1,), (1,)), ((), ())), preferred_element_type=F32)
    q3 = qt.reshape(M_HEADS, M_DIM, tm)
    ms = jnp.mean(q3 * q3, axis=1, keepdims=True)
    q3 = q3 * lax.rsqrt(ms + EPS) * qg_ref[...][None] * (M_DIM ** -0.5 * LOG2E)
    q = q3.reshape(M_HEADS * M_DIM, tm).astype(BF16)

    outs = []
    for hd in range(M_HEADS):
        rows = slice(hd * M_DIM, (hd + 1) * M_DIM)
        s = _dot(k_ref[:, rows], q[rows, :])
        p = jnp.exp2(s - jnp.max(s, axis=0, keepdims=True))
        outs.append(_normalised(_dot(_with_ones(vt_ref[0, rows, :]), p.astype(BF16)), M_DIM))
    o = jnp.concatenate(outs, axis=0)
    o_ref[...] = h1 + _dot(o.T.astype(BF16), wo_ref[...])


def _mix_out_mem(h, ots, ws, g, wq, qg, k, v_t, wo):
    s, d = h.shape
    tm = min(ROW_TILE, s)
    n = len(ots)
    const2 = lambda i: (0, 0)
    in_specs = [pl.BlockSpec((tm, d), lambda i: (i, 0))]
    in_specs += [pl.BlockSpec((ot.shape[0], tm), lambda i: (0, i)) for ot in ots]
    in_specs += [pl.BlockSpec(w.shape, const2) for w in ws]
    in_specs += [
        pl.BlockSpec((1, d), const2),
        pl.BlockSpec((wq.shape[1], d), const2),
        pl.BlockSpec((M_DIM, 1), const2),
        pl.BlockSpec(k.shape, const2),
        pl.BlockSpec(v_t.shape, lambda i: (0, 0, 0)),
        pl.BlockSpec(wo.shape, const2),
    ]
    return pl.pallas_call(
        functools.partial(_mix_out_mem_body, n),
        grid=(s // tm,),
        in_specs=in_specs,
        out_specs=pl.BlockSpec((tm, d), lambda i: (i, 0)),
        out_shape=jax.ShapeDtypeStruct((s, d), F32),
        compiler_params=_cparams(1),
        name="mix_out_mem",
    )(h, *ots, *[w.astype(BF16) for w in ws], g.reshape(1, d), wq.T.astype(BF16),
      qg.astype(F32).reshape(M_DIM, 1), k, v_t, wo.astype(BF16))


def _softmax_step_ref(s_ref, stream, vts, m_ref, acc_ref):
    m_old = m_ref[stream]
    m_new = jnp.maximum(m_old, jnp.max(s_ref[stream], axis=0, keepdims=True))
    pv = None
    for u in range(KC):
        p = jnp.exp2(s_ref[stream, u * TK:(u + 1) * TK, :] - m_new)
        d = _dot(vts[u], p.astype(BF16))
        pv = d if pv is None else pv + d
    acc_ref[stream] = jnp.exp2(m_old - m_new) * acc_ref[stream] + pv
    m_ref[stream] = m_new


def _with_ones(vt):
    return jnp.concatenate([vt, jnp.ones((ROWSUM_ROWS, vt.shape[1]), vt.dtype)], axis=0)


def _normalised(acc, dv):
    return acc[:dv] / acc[dv:dv + 1]


def _score_pipeline(i, n_chunks, k_ref, q_halves, bias_ref, heads, mask_fn, consume,
                    s_even, s_odd):
    def produce(near, c, s_ref):
        for u in range(KC):
            j = c * KC + u
            k0 = pl.multiple_of(j * TK, TK)
            kt = k_ref[pl.ds(k0, TK), :]
            dist = i - j
            which = jnp.where(dist < 0, 3, jnp.minimum(dist, 2))
            sel = None if mask_fn is None else mask_fn(k0)
            for a, qh in enumerate(q_halves):
                s = _dot(kt, qh)
                if near:
                    s = s + bias_ref[heads[a], which]
                s_ref[a, u * TK:(u + 1) * TK, :] = s if sel is None else jnp.where(sel, s, NEG)

    def pair(near, p, carry):
        c = 2 * p
        produce(near, c + 1, s_odd)
        consume(c, s_even)
        produce(near, c + 2, s_even)
        consume(c + 1, s_odd)
        return carry

    n_far = jnp.maximum(i - 1, 0) // KC
    far_pairs = jnp.maximum(n_far - 1, 0) // 2
    full_pairs = (n_chunks - 1) // 2
    produce(True, 0, s_even)
    lax.fori_loop(0, far_pairs, functools.partial(pair, False), 0)
    lax.fori_loop(far_pairs, full_pairs, functools.partial(pair, True), 0)

    @pl.when(n_chunks % 2 == 1)
    def _last_one():
        consume(n_chunks - 1, s_even)

    @pl.when(n_chunks % 2 == 0)
    def _last_two():
        produce(True, n_chunks - 1, s_odd)
        consume(n_chunks - 2, s_even)
        consume(n_chunks - 1, s_odd)


def _init_state(m_ref, acc_ref):
    m_ref[...] = jnp.full(m_ref.shape, NEG, F32)
    acc_ref[...] = jnp.zeros_like(acc_ref)


def _half_masked(q):
    head = lax.broadcasted_iota(I32, q.shape, 0) // HEAD_DIM
    zero = jnp.zeros_like(q)
    return tuple(jnp.where(head == n, q, zero) for n in range(q.shape[0] // HEAD_DIM))


def _t5_bucket(rel):
    nb = T5_BUCKETS // 2
    max_exact = nb // 2
    offset = (rel < 0).astype(jnp.int32) * nb
    n = jnp.abs(rel)
    nf = jnp.maximum(n, 1).astype(jnp.float32)
    large = max_exact + (jnp.log(nf / max_exact) / math.log(T5_MAX_DIST / max_exact)
                         * (nb - max_exact)).astype(jnp.int32)
    large = jnp.minimum(large, nb - 1)
    return offset + jnp.where(n < max_exact, n, large)


def _toeplitz_body(masks, fills, n_keys, n_q, w_ref, o_ref):
    sk = lax.broadcasted_iota(I32, (n_keys, n_q), 0)
    tq = lax.broadcasted_iota(I32, (n_keys, n_q), 1)
    for blk, mask in enumerate(masks):
        w = jnp.broadcast_to(w_ref[0, blk], (n_keys, w_ref.shape[-1]))
        t = pltpu.roll(w, 0, 1, stride=1, stride_axis=0)[:, :n_q]
        o_ref[0, blk] = t if mask is None else jnp.where(mask(sk, tq), t, NEG)
    for n, fill in enumerate(fills):
        o_ref[0, len(masks) + n] = jnp.full((n_keys, n_q), fill, F32)


def _toeplitz_tiles(u, n_keys, n_q, masks, fills=()):
    hn, nb, _ = u.shape
    width = pl.cdiv(n_keys + n_q - 1, 128) * 128
    w = jnp.concatenate([u[..., n_keys - 1:],
                         jnp.zeros((hn, nb, width - (n_keys + n_q - 1)), F32),
                         u[..., :n_keys - 1]], axis=-1)
    n_out = nb + len(fills)
    return pl.pallas_call(
        functools.partial(_toeplitz_body, tuple(masks), tuple(fills), n_keys, n_q),
        grid=(hn,),
        in_specs=[pl.BlockSpec((1, nb, 1, width), lambda h: (h, 0, 0, 0))],
        out_specs=pl.BlockSpec((1, n_out, n_keys, n_q), lambda h: (h, 0, 0, 0)),
        out_shape=jax.ShapeDtypeStruct((hn, n_out, n_keys, n_q), F32),
        compiler_params=_cparams(1),
        name="toeplitz_tiles",
    )(w.reshape(hn, nb, 1, width))


def _t5_bias_tiles(table):
    table = table.astype(F32)
    far = table[_t5_bucket(jnp.full((1,), 2 * TK, jnp.int32))].T
    r = jnp.arange(-(TK - 1), TQ, dtype=jnp.int32)
    u = jnp.stack([(table[_t5_bucket(dist * TK + r)].T - far) * LOG2E for dist in range(2)],
                  axis=1)
    chunk_causal = lambda sk, tq: (sk // CHUNK) <= (tq // CHUNK)
    return _toeplitz_tiles(u, TK, TQ, (chunk_causal, None), fills=(0.0, NEG))


def _key_to_f32(key):
    return lax.bitcast_convert_type(jnp.where(key < 0, INT_MIN - key, key), F32)


def _score_codes(x):
    b = lax.bitcast_convert_type(x, I32)
    b = jnp.where((b & 0x7FFFFFFF) < MIN_NORMAL_BITS, 0, b)
    key = jnp.where(b < 0, INT_MIN - b, b)
    fb = jnp.where(b < 0, b + 0xFFFF, b) & jnp.int32(-65536)
    return key, lax.bitcast_convert_type(fb, F32).astype(BF16)


def _tree_sum(x):
    while x.shape[0] > 1:
        half = x.shape[0] // 2
        x = x[:half] + x[half:]
    return x[0]


def _select_topk(keys_ref, hb_ref, thr_ref, n_chunks, top_k, idx_bits):
    rows = KC * TK
    one, zero = jnp.ones((), BF16), jnp.zeros((), BF16)

    def count16(cand):
        cand = jnp.where((cand > 0) & (cand < MIN_NORMAL_BITS), MIN_NORMAL_BITS, cand)
        tb = _key_to_f32(cand).astype(BF16)

        def body(c, acc):
            c0 = pl.multiple_of(c * rows, rows)
            ge = jnp.where(hb_ref[pl.ds(c0, rows), :] >= tb, one, zero)
            return acc + _tree_sum(ge.reshape(rows // 16, 16, TQ)).astype(F32)

        acc = lax.fori_loop(0, n_chunks, body, jnp.zeros((16, TQ), F32))
        return jnp.sum(acc, axis=0, keepdims=True).astype(I32)

    def count32(pred):
        def body(c, acc):
            c0 = pl.multiple_of(c * rows, rows)
            hit = pred(keys_ref[pl.ds(c0, rows), :], c0).astype(I32)
            return acc + jnp.sum(hit.reshape(rows // 8, 8, TQ), axis=0)

        acc = lax.fori_loop(0, n_chunks, body, jnp.zeros((8, TQ), I32))
        return jnp.sum(acc, axis=0, keepdims=True)

    def search(count, bits, thr, c_thr, n_fixed):
        def step(carry):
            b, thr, c_thr = carry
            cand = thr + lax.shift_left(jnp.int32(1), bits[0] - b)
            cnt = count(cand)
            ok = cnt >= top_k
            return b + 1, jnp.where(ok, cand, thr), jnp.where(ok, cnt, c_thr)

        def unresolved(carry):
            b, _, c_thr = carry
            return (b <= bits[0] - bits[1]) & (jnp.max(c_thr) > top_k)

        carry = lax.fori_loop(0, n_fixed, lambda _, c: step(c), (jnp.int32(0), thr, c_thr))
        return lax.while_loop(unresolved, step, carry)[1:]

    cnt = count16(jnp.zeros((1, TQ), I32))
    ok = cnt >= top_k
    thr = jnp.where(ok, 0, INT_MIN).astype(I32)
    c_thr = jnp.where(ok, cnt, 2 ** 30)
    thr, c_thr = search(count16, (30, 16), thr, c_thr, 15)
    thr, c_thr = search(lambda cand: count32(lambda keys, c0: keys >= cand), (15, 0), thr, c_thr, 11)
    thr_ref[...] = jnp.maximum(thr, INT_MIN + 1)

    split = (c_thr > top_k) & (thr > INT_MIN)

    @pl.when(jnp.max(split.astype(I32)) > 0)
    def _break_ties():
        need = top_k - count32(lambda keys, c0: keys > thr)

        def equal_before(limit):
            def pred(keys, c0):
                idx = c0 + lax.broadcasted_iota(I32, keys.shape, 0)
                return (keys == thr) & (idx < limit)
            return count32(pred)

        def step(b, q):
            cand = q + lax.shift_left(jnp.int32(1), idx_bits - 1 - b)
            return jnp.where(equal_before(cand) < need, cand, q)

        q = lax.fori_loop(0, idx_bits, step, jnp.zeros((1, TQ), I32))

        def lower(c, carry):
            c0 = pl.multiple_of(c * rows, rows)
            keys = keys_ref[pl.ds(c0, rows), :]
            idx = c0 + lax.broadcasted_iota(I32, keys.shape, 0)
            drop = split & (keys == thr) & (idx > q)
            keys_ref[pl.ds(c0, rows), :] = jnp.where(drop, thr - 1, keys)
            return carry

        lax.fori_loop(0, n_chunks, lower, 0)


def _dsa_body(top_k, idx_bits, iq_ref, iw_ref, ik_ref, q_ref, k_ref, vt_ref, bias_ref, o_ref,
              keys_ref, hb_ref, thr_ref, m_ref, acc_ref, s_even, s_odd):
    i = pl.program_id(0)
    g = pl.program_id(1)
    n_tiles = i + 1
    n_chunks = (i + KC) // KC

    @pl.when(g == 0)
    def _select():
        w_all = iw_ref[...]
        zpad = jnp.zeros((IDX_DIM, TQ), BF16)
        qz = [jnp.concatenate([iq_ref[h * IDX_DIM:(h + 1) * IDX_DIM, :], zpad], axis=0)
              for h in range(IDX_HEADS)]

        def score_tiles(p, carry):
            for u in range(2):
                k0 = pl.multiple_of((2 * p + u) * TK, TK)
                ikt = ik_ref[pl.ds(k0, TK), :]
                sc = jnp.zeros((TK, TQ), F32)
                for h in range(IDX_HEADS):
                    sc = sc + w_all[h:h + 1, :] * jnp.maximum(_dot(ikt, qz[h]), 0.0)
                keys_ref[pl.ds(k0, TK), :], hb_ref[pl.ds(k0, TK), :] = _score_codes(sc)
            return carry

        lax.fori_loop(0, (n_tiles + 1) // 2, score_tiles, 0)

        d0 = pl.multiple_of(i * TK, TK)
        sk = lax.broadcasted_iota(I32, (TK, TQ), 0) // CHUNK
        tq = lax.broadcasted_iota(I32, (TK, TQ), 1) // CHUNK
        no_key = jnp.full((TK, TQ), INT_MIN, I32)
        no_hb = jnp.full((TK, TQ), jnp.nan, BF16)
        keys_ref[pl.ds(d0, TK), :] = jnp.where(sk <= tq, keys_ref[pl.ds(d0, TK), :], no_key)
        hb_ref[pl.ds(d0, TK), :] = jnp.where(sk <= tq, hb_ref[pl.ds(d0, TK), :], no_hb)

        def fill_tile(j, carry):
            k0 = pl.multiple_of(j * TK, TK)
            keys_ref[pl.ds(k0, TK), :] = no_key
            hb_ref[pl.ds(k0, TK), :] = no_hb
            return carry

        lax.fori_loop(n_tiles, n_chunks * KC, fill_tile, 0)
        _select_topk(keys_ref, hb_ref, thr_ref, n_chunks, top_k, idx_bits)

    _init_state(m_ref, acc_ref)
    q_halves = _half_masked(q_ref[...])
    thr = thr_ref[...]

    def selected(k0):
        return keys_ref[pl.ds(k0, TK), :] >= thr

    def consume(c, s_ref):
        for a in range(2):
            vts = [_with_ones(vt_ref[c * KC + u, a * HEAD_DIM:(a + 1) * HEAD_DIM, :])
                   for u in range(KC)]
            _softmax_step_ref(s_ref, a, vts, m_ref, acc_ref)

    _score_pipeline(i, n_chunks, k_ref, q_halves, bias_ref, (0, 1), selected, consume,
                    s_even, s_odd)
    for a in range(2):
        o_ref[a * HEAD_DIM:(a + 1) * HEAD_DIM, :] = _normalised(acc_ref[a], HEAD_DIM)


def _dsa_attention(iq_t, iw_t, ik, q_t, k, v_t, bias, top_k):
    s = k.shape[0]
    n_pairs = A_HEADS // 2
    return pl.pallas_call(
        functools.partial(_dsa_body, top_k, (s - 1).bit_length()),
        grid=(s // TQ, n_pairs),
        in_specs=[
            pl.BlockSpec((IDX_HEADS * IDX_DIM, TQ), lambda i, g: (0, i)),
            pl.BlockSpec((IDX_HEADS, TQ), lambda i, g: (0, i)),
            pl.BlockSpec((s, 128), lambda i, g: (0, 0)),
            pl.BlockSpec((128, TQ), lambda i, g: (g, i)),
            pl.BlockSpec((s, 128), lambda i, g: (0, g)),
            pl.BlockSpec((s // TK, 128, TK), lambda i, g: (0, g, 0)),
            pl.BlockSpec((2, 4, TK, TQ), lambda i, g: (g, 0, 0, 0)),
        ],
        out_specs=pl.BlockSpec((128, TQ), lambda i, g: (g, i)),
        out_shape=jax.ShapeDtypeStruct((A_HEADS * HEAD_DIM, s), F32),
        scratch_shapes=[
            pltpu.VMEM((s, TQ), I32),
            pltpu.VMEM((s, TQ), BF16),
            pltpu.VMEM((1, TQ), I32),
            pltpu.VMEM((2, 1, TQ), F32),
            pltpu.VMEM((2, HEAD_DIM + ROWSUM_ROWS, TQ), F32),
            pltpu.VMEM((2, KC * TK, TQ), F32),
            pltpu.VMEM((2, KC * TK, TQ), F32),
        ],
        compiler_params=_cparams(2),
        name="dsa_attention",
    )(iq_t, iw_t, ik, q_t, k, v_t, bias)


def _diff_body(lambda_init, q_ref, k_ref, vt_ref, bias_ref, lam_ref, subln_ref, o_ref,
               m_ref, acc_ref, s_even, s_odd):
    i = pl.program_id(0)
    _init_state(m_ref, acc_ref)
    q_halves = _half_masked(q_ref[...])

    def consume(c, s_ref):
        for hd in range(DIFF_HEADS):
            rows = slice(hd * B_VDIM, (hd + 1) * B_VDIM)
            vts = [_with_ones(vt_ref[c * KC + u, rows, :]) for u in range(KC)]
            for m in range(2):
                _softmax_step_ref(s_ref, 2 * hd + m, vts, m_ref, acc_ref)

    heads = tuple(hd for hd in range(DIFF_HEADS) for _ in range(2))
    _score_pipeline(i, (i + KC) // KC, k_ref, q_halves, bias_ref, heads, None, consume,
                    s_even, s_odd)

    lq1, lk1, lq2, lk2 = (lam_ref[r:r + 1, :] for r in range(4))
    lam = (jnp.exp(jnp.sum(lq1 * lk1, keepdims=True)) - jnp.exp(jnp.sum(lq2 * lk2, keepdims=True))
           + lambda_init)
    for hd in range(DIFF_HEADS):
        o = (_normalised(acc_ref[2 * hd], B_VDIM)
             - lam * _normalised(acc_ref[2 * hd + 1], B_VDIM))
        ms = jnp.mean(o * o, axis=0, keepdims=True)
        o_ref[hd * B_VDIM:(hd + 1) * B_VDIM, :] = (o * lax.rsqrt(ms + EPS) * subln_ref[...]
                                                   * (1.0 - lambda_init))


def _diff_attention(q_t, k, v_t, bias, lam_rows, subln, lambda_init):
    s = k.shape[0]
    f = DIFF_HEADS * B_VDIM
    n_streams = 2 * DIFF_HEADS
    return pl.pallas_call(
        functools.partial(_diff_body, lambda_init),
        grid=(s // TQ, B_HEADS // DIFF_HEADS),
        in_specs=[
            pl.BlockSpec((f, TQ), lambda i, h: (h, i)),
            pl.BlockSpec((s, f), lambda i, h: (0, h)),
            pl.BlockSpec((s // TK, f, TK), lambda i, h: (0, h, 0)),
            pl.BlockSpec((DIFF_HEADS, 4, TK, TQ), lambda i, h: (h, 0, 0, 0)),
            pl.BlockSpec((4, HEAD_DIM), lambda i, h: (0, 0)),
            pl.BlockSpec((B_VDIM, 1), lambda i, h: (0, 0)),
        ],
        out_specs=pl.BlockSpec((f, TQ), lambda i, h: (h, i)),
        out_shape=jax.ShapeDtypeStruct((B_HEADS * B_VDIM, s), F32),
        scratch_shapes=[
            pltpu.VMEM((n_streams, 1, TQ), F32),
            pltpu.VMEM((n_streams, B_VDIM + ROWSUM_ROWS, TQ), F32),
            pltpu.VMEM((n_streams, KC * TK, TQ), F32),
            pltpu.VMEM((n_streams, KC * TK, TQ), F32),
        ],
        compiler_params=_cparams(2),
        name="diff_attention",
    )(q_t, k, v_t, bias, lam_rows, subln.astype(F32).reshape(B_VDIM, 1))


def _band_bias_tiles(rel_bias):
    table = rel_bias.astype(F32) * LOG2E
    r = jnp.arange(-(BAND_T - 1), BAND_T, dtype=jnp.int32)
    u = jnp.stack([table[jnp.clip((1 - blk) * BAND_T + r, -REL_CLIP, REL_CLIP) + REL_CLIP].T
                   for blk in range(2)], axis=1)

    def in_band(blk):
        def mask(sk, tq):
            kc = sk // CHUNK + blk * (BAND_T // CHUNK)
            qc = tq // CHUNK + BAND_T // CHUNK
            return (kc <= qc) & (kc >= qc - (C_BAND - 1))
        return mask

    return _toeplitz_tiles(u, BAND_T, BAND_T, (in_band(0), in_band(1)))


def _band_body(q_ref, kp_ref, kc_ref, vp_ref, vc_ref, bias_ref, o_ref):
    i = pl.program_id(1)
    q_halves = _half_masked(q_ref[...])
    for a in range(len(q_halves)):
        rows = slice(a * HEAD_DIM, (a + 1) * HEAD_DIM)
        s_prev = _dot(kp_ref[...], q_halves[a]) + jnp.where(i > 0, bias_ref[a, 0], NEG)
        s_cur = _dot(kc_ref[...], q_halves[a]) + bias_ref[a, 1]
        m = jnp.maximum(jnp.max(s_prev, axis=0, keepdims=True),
                        jnp.max(s_cur, axis=0, keepdims=True))
        acc = (_dot(_with_ones(vp_ref[0, rows, :]), jnp.exp2(s_prev - m).astype(BF16))
               + _dot(_with_ones(vc_ref[0, rows, :]), jnp.exp2(s_cur - m).astype(BF16)))
        o_ref[rows, :] = _normalised(acc, HEAD_DIM)


def _band_attention(q_t, k, v_t, bias):
    s = k.shape[0]
    t = BAND_T
    f = BAND_HEADS * HEAD_DIM
    prev = lambda i: jnp.maximum(i - 1, 0)
    return pl.pallas_call(
        _band_body,
        grid=(C_HEADS // BAND_HEADS, s // t),
        in_specs=[
            pl.BlockSpec((f, t), lambda p, i: (p, i)),
            pl.BlockSpec((t, f), lambda p, i: (prev(i), p)),
            pl.BlockSpec((t, f), lambda p, i: (i, p)),
            pl.BlockSpec((1, f, t), lambda p, i: (prev(i), p, 0)),
            pl.BlockSpec((1, f, t), lambda p, i: (i, p, 0)),
            pl.BlockSpec((BAND_HEADS, 2, t, t), lambda p, i: (p, 0, 0, 0)),
        ],
        out_specs=pl.BlockSpec((f, t), lambda p, i: (p, i)),
        out_shape=jax.ShapeDtypeStruct((C_HEADS * HEAD_DIM, s), F32),
        compiler_params=_cparams(2),
        name="band_attention",
    )(q_t, k, k, v_t, v_t, bias)


def _even_mixer(h, mix_g, w_in, a_qg, a_kg, idx_kg, b_qg, b_kg, lq1, lk1, lq2, lk2, b_subln,
                w_out, t5_bias, layer_idx):
    s = h.shape[0]
    sizes = [A_HEADS * HEAD_DIM] * 3 + [IDX_HEADS * IDX_DIM, IDX_DIM, IDX_HEADS] + \
            [B_HEADS * 2 * HEAD_DIM] * 2 + [B_HEADS * B_VDIM]
    cuts = np.cumsum([0] + sizes)
    w = [w_in[:, cuts[n]:cuts[n + 1]] for n in range(len(sizes))]
    qk_scale = HEAD_DIM ** -0.5 * LOG2E
    segs = [
        _Seg(w[0], HEAD_DIM, a_qg, qk_scale, "fm"),
        _Seg(w[1], HEAD_DIM, a_kg, 1.0, "tm"),
        _Seg(w[2], layout="vt"),
        _Seg(w[3], scale=IDX_DIM ** -0.5, layout="fm"),
        _Seg(w[4], IDX_DIM, idx_kg, 1.0, "tm", pad=IDX_DIM),
        _Seg(w[5], scale=IDX_HEADS ** -0.5, layout="fm", dtype=F32),
        _Seg(w[6], HEAD_DIM, b_qg, qk_scale, "fm"),
        _Seg(w[7], HEAD_DIM, b_kg, 1.0, "tm"),
        _Seg(w[8], layout="vt"),
    ]
    aq_t, ak, av_t, iq_t, ik, iw_t, bq_t, bk, bv_t = _project(h, mix_g, segs)
    bias = _t5_bias_tiles(t5_bias)
    assert s % (KC * TK) == 0
    top_k = min(TOPK_MAX, s // 4)
    out_a = _dsa_attention(iq_t, iw_t, ik, aq_t, ak, av_t, bias[:A_HEADS], top_k)
    lambda_init = 0.8 - 0.6 * math.exp(-0.3 * layer_idx)
    lam_rows = jnp.stack([lq1, lk1, lq2, lk2]).astype(F32)
    out_b = _diff_attention(bq_t, bk, bv_t, bias[A_HEADS:], lam_rows, b_subln, lambda_init)
    na = A_HEADS * HEAD_DIM
    return [out_a, out_b], [w_out[:na], w_out[na:]]


def _odd_mixer(h, mix_g, w_in, c_qg, c_kg, rel_bias, w_out):
    f = C_HEADS * HEAD_DIM
    segs = [
        _Seg(w_in[:, :f], HEAD_DIM, c_qg, HEAD_DIM ** -0.5 * LOG2E, "fm"),
        _Seg(w_in[:, f:2 * f], HEAD_DIM, c_kg, 1.0, "tm"),
        _Seg(w_in[:, 2 * f:], layout="vt", tile=BAND_T),
    ]
    q_t, k, v_t = _project(h, mix_g, segs)
    out = _band_attention(q_t, k, v_t, _band_bias_tiles(rel_bias))
    return [out], [w_out]


def _mixer_out_and_memory(h, mixer_out, mem, mg, sg, wq, wkv, qg, kg, wo):
    ots, ws = mixer_out
    f = M_HEADS * M_DIM
    k, v_t = _project(mem, sg, [_Seg(wkv[:, :f], M_DIM, kg, 1.0, "tm"),
                                _Seg(wkv[:, f:], layout="vt", tile=mem.shape[0])])
    return _mix_out_mem(h, ots, ws, mg, wq, qg, k, v_t, wo)


def kernel(x, mem, t5_bias,
           l0_ffn1_norm, l0_ffn1_wg, l0_ffn1_wu, l0_ffn1_wd,
           l0_mix_norm, l0_w_in, l0_a_q_norm, l0_a_k_norm, l0_idx_k_norm,
           l0_b_q_norm, l0_b_k_norm, l0_b_lq1, l0_b_lk1, l0_b_lq2, l0_b_lk2, l0_b_subln, l0_w_out,
           l0_mem_norm, l0_mem_src_norm, l0_mem_wq, l0_mem_wkv, l0_mem_q_norm, l0_mem_k_norm, l0_mem_wo,
           l0_ffn2_norm, l0_ffn2_wg, l0_ffn2_wu, l0_ffn2_wd,
           l1_ffn1_norm, l1_ffn1_wg, l1_ffn1_wu, l1_ffn1_wd,
           l1_mix_norm, l1_w_in, l1_c_q_norm, l1_c_k_norm, l1_c_rel_bias, l1_w_out,
           l1_mem_norm, l1_mem_src_norm, l1_mem_wq, l1_mem_wkv, l1_mem_q_norm, l1_mem_k_norm, l1_mem_wo,
           l1_ffn2_norm, l1_ffn2_wg, l1_ffn2_wu, l1_ffn2_wd):
    bsz, seq, d = x.shape
    assert bsz == 1 and mem.shape[0] == 1
    h = x.reshape(seq, d)
    m = mem.reshape(mem.shape[1], d)

    h = _ffn(h, l0_ffn1_norm, l0_ffn1_wg, l0_ffn1_wu, l0_ffn1_wd)
    mixed = _even_mixer(h, l0_mix_norm, l0_w_in, l0_a_q_norm, l0_a_k_norm, l0_idx_k_norm,
                        l0_b_q_norm, l0_b_k_norm, l0_b_lq1, l0_b_lk1, l0_b_lq2, l0_b_lk2,
                        l0_b_subln, l0_w_out, t5_bias, 0)
    h = _mixer_out_and_memory(h, mixed, m, l0_mem_norm, l0_mem_src_norm, l0_mem_wq, l0_mem_wkv,
                              l0_mem_q_norm, l0_mem_k_norm, l0_mem_wo)
    h = _ffn(h, l0_ffn2_norm, l0_ffn2_wg, l0_ffn2_wu, l0_ffn2_wd)

    h = _ffn(h, l1_ffn1_norm, l1_ffn1_wg, l1_ffn1_wu, l1_ffn1_wd)
    mixed = _odd_mixer(h, l1_mix_norm, l1_w_in, l1_c_q_norm, l1_c_k_norm, l1_c_rel_bias, l1_w_out)
    h = _mixer_out_and_memory(h, mixed, m, l1_mem_norm, l1_mem_src_norm, l1_mem_wq, l1_mem_wkv,
                              l1_mem_q_norm, l1_mem_k_norm, l1_mem_wo)
    h = _ffn(h, l1_ffn2_norm, l1_ffn2_wg, l1_ffn2_wu, l1_ffn2_wd)
    return h.reshape(bsz, seq, d)
```

```python
import functools
import math

import jax
import jax.numpy as jnp
import numpy as np
from jax import lax
from jax.experimental import pallas as pl
from jax.experimental.pallas import tpu as pltpu

D_MODEL = 1024
CHUNK = 64
HEAD_DIM = 64
EPS = 1e-6
A_HEADS = 8
IDX_HEADS = 8
IDX_DIM = 64
TOPK_MAX = 256
B_VDIM = 128
B_HEADS = 4
C_HEADS = 16
C_BAND = 9
REL_CLIP = 256
T5_BUCKETS = 32
T5_MAX_DIST = 128
M_HEADS = 4
M_DIM = 128
D_FF = 2816

F32 = jnp.float32
BF16 = jnp.bfloat16
I32 = jnp.int32

NEG = -1e30
LOG2E = math.log2(math.e)
INT_MIN = -(2 ** 31)
MIN_NORMAL_BITS = 0x00800000
VMEM_LIMIT = 60 * 1024 * 1024

ROW_TILE = 512
FF_CHUNK = 256
TQ = 256
TK = 256
KC = 4
ROWSUM_ROWS = 16
BAND_T = 512
BAND_HEADS = 4
DIFF_HEADS = 1


def _cparams(n_axes):
    return pltpu.CompilerParams(dimension_semantics=("arbitrary",) * n_axes,
                                vmem_limit_bytes=VMEM_LIMIT)


def _dot(a, b):
    return jnp.dot(a, b, preferred_element_type=F32)


def _rms_rows(x, g):
    ms = jnp.mean(x * x, axis=-1, keepdims=True)
    return x * lax.rsqrt(ms + EPS) * g


def _ffn_body(h_ref, g_ref, wg_ref, wu_ref, wd_ref, o_ref, xn_ref, acc_ref):
    x = h_ref[...]
    xn_ref[...] = _rms_rows(x, g_ref[...]).astype(BF16)
    acc_ref[...] = jnp.zeros_like(acc_ref)

    def chunk(c, carry):
        xn = xn_ref[...]
        cols = pl.ds(pl.multiple_of(c * FF_CHUNK, FF_CHUNK), FF_CHUNK)
        gate = _dot(xn, wg_ref[:, cols])
        up = _dot(xn, wu_ref[:, cols])
        act = (gate * jax.nn.sigmoid(gate) * up).astype(BF16)
        acc_ref[...] += _dot(act, wd_ref[c])
        return carry

    lax.fori_loop(0, wd_ref.shape[0], chunk, 0)
    o_ref[...] = x + 0.5 * acc_ref[...]


def _ffn(h, g, wg, wu, wd):
    s, d = h.shape
    ff = wg.shape[1]
    nc = ff // FF_CHUNK
    wd3 = wd.astype(BF16).reshape(nc, FF_CHUNK, d)
    tm = min(ROW_TILE, s)
    const2 = lambda i: (0, 0)
    return pl.pallas_call(
        _ffn_body,
        grid=(s // tm,),
        in_specs=[
            pl.BlockSpec((tm, d), lambda i: (i, 0)),
            pl.BlockSpec((1, d), lambda i: (0, 0)),
            pl.BlockSpec((d, ff), const2, pipeline_mode=pl.Buffered(1)),
            pl.BlockSpec((d, ff), const2, pipeline_mode=pl.Buffered(1)),
            pl.BlockSpec((nc, FF_CHUNK, d), lambda i: (0, 0, 0), pipeline_mode=pl.Buffered(1)),
        ],
        out_specs=pl.BlockSpec((tm, d), lambda i: (i, 0)),
        out_shape=jax.ShapeDtypeStruct((s, d), F32),
        scratch_shapes=[pltpu.VMEM((tm, d), BF16), pltpu.VMEM((tm, d), F32)],
        compiler_params=_cparams(1),
        name="ffn",
    )(h, g.reshape(1, d), wg.astype(BF16), wu.astype(BF16), wd3)


class _Seg:
    def __init__(self, w, head_dim=None, gain=None, scale=1.0, layout="fm",
                 dtype=BF16, tile=TK, pad=0):
        self.w = w
        self.head_dim = head_dim
        self.gain = gain
        self.scale = scale
        self.layout = layout
        self.dtype = dtype
        self.tile = tile
        self.pad = pad


def _proj_body(segs, x_ref, g_ref, *refs):
    n = len(segs)
    w_refs = refs[:n]
    gains = [r for r in refs[n:2 * n]]
    outs = refs[2 * n:]
    xn = _rms_rows(x_ref[...], g_ref[...]).astype(BF16)
    tm = xn.shape[0]
    for seg, w_ref, gain_ref, o_ref in zip(segs, w_refs, gains, outs):
        yt = lax.dot_general(w_ref[...], xn, (((1,), (1,)), ((), ())),
                             preferred_element_type=F32)
        f = yt.shape[0]
        if seg.head_dim is not None:
            hd = seg.head_dim
            y3 = yt.reshape(f // hd, hd, tm)
            ms = jnp.mean(y3 * y3, axis=1, keepdims=True)
            y3 = y3 * lax.rsqrt(ms + EPS) * gain_ref[...][None]
            yt = y3.reshape(f, tm)
        if seg.scale != 1.0:
            yt = yt * seg.scale
        if seg.pad:
            yt = jnp.concatenate([yt, jnp.zeros((seg.pad, tm), F32)], axis=0)
        if seg.layout == "fm":
            o_ref[...] = yt.astype(seg.dtype)
        elif seg.layout == "tm":
            o_ref[...] = yt.T.astype(seg.dtype)
        else:
            for t in range(tm // seg.tile):
                o_ref[t] = yt[:, t * seg.tile:(t + 1) * seg.tile].astype(seg.dtype)


def _project(x, g, segs):
    s, d = x.shape
    tm = min(ROW_TILE, s)
    in_specs = [pl.BlockSpec((tm, d), lambda i: (i, 0)),
                pl.BlockSpec((1, d), lambda i: (0, 0))]
    args = [x, g.reshape(1, d)]
    for seg in segs:
        wt = seg.w.T.astype(BF16)
        args.append(wt)
        in_specs.append(pl.BlockSpec(wt.shape, lambda i: (0, 0)))
    for seg in segs:
        hd = seg.head_dim or 8
        gain = seg.gain if seg.gain is not None else jnp.ones((hd,), F32)
        args.append(gain.astype(F32).reshape(hd, 1))
        in_specs.append(pl.BlockSpec((hd, 1), lambda i: (0, 0)))
    out_shapes, out_specs = [], []
    for seg in segs:
        f = seg.w.shape[1] + seg.pad
        if seg.layout == "fm":
            out_shapes.append(jax.ShapeDtypeStruct((f, s), seg.dtype))
            out_specs.append(pl.BlockSpec((f, tm), lambda i: (0, i)))
        elif seg.layout == "tm":
            out_shapes.append(jax.ShapeDtypeStruct((s, f), seg.dtype))
            out_specs.append(pl.BlockSpec((tm, f), lambda i: (i, 0)))
        else:
            nt = tm // seg.tile
            out_shapes.append(jax.ShapeDtypeStruct((s // seg.tile, f, seg.tile), seg.dtype))
            out_specs.append(pl.BlockSpec((nt, f, seg.tile), lambda i: (i, 0, 0)))
    return pl.pallas_call(
        functools.partial(_proj_body, segs),
        grid=(s // tm,),
        in_specs=in_specs,
        out_specs=out_specs,
        out_shape=out_shapes,
        compiler_params=_cparams(1),
        name="project",
    )(*args)


def _mix_out_mem_body(n, h_ref, *refs):
    ot_refs, w_refs = refs[:n], refs[n:2 * n]
    g_ref, wq_ref, qg_ref, k_ref, vt_ref, wo_ref, o_ref = refs[2 * n:]
    h1 = h_ref[...]
    for ot_ref, w_ref in zip(ot_refs, w_refs):
        h1 = h1 + _dot(ot_ref[...].T.astype(BF16), w_ref[...])
    tm = h1.shape[0]

    xn = _rms_rows(h1, g_ref[...]).astype(BF16)
    qt = lax.dot_general(wq_ref[...], xn, (((1,), (1,)), ((), ())), preferred_element_type=F32)
    q3 = qt.reshape(M_HEADS, M_DIM, tm)
    ms = jnp.mean(q3 * q3, axis=1, keepdims=True)
    q3 = q3 * lax.rsqrt(ms + EPS) * qg_ref[...][None] * (M_DIM ** -0.5 * LOG2E)
    q = q3.reshape(M_HEADS * M_DIM, tm).astype(BF16)

    outs = []
    for hd in range(M_HEADS):
        rows = slice(hd * M_DIM, (hd + 1) * M_DIM)
        s = _dot(k_ref[:, rows], q[rows, :])
        p = jnp.exp2(s - jnp.max(s, axis=0, keepdims=True))
        outs.append(_normalised(_dot(_with_ones(vt_ref[0, rows, :]), p.astype(BF16)), M_DIM))
    o = jnp.concatenate(outs, axis=0)
    o_ref[...] = h1 + _dot(o.T.astype(BF16), wo_ref[...])


def _mix_out_mem(h, ots, ws, g, wq, qg, k, v_t, wo):
    s, d = h.shape
    tm = min(ROW_TILE, s)
    n = len(ots)
    const2 = lambda i: (0, 0)
    in_specs = [pl.BlockSpec((tm, d), lambda i: (i, 0))]
    in_specs += [pl.BlockSpec((ot.shape[0], tm), lambda i: (0, i)) for ot in ots]
    in_specs += [pl.BlockSpec(w.shape, const2) for w in ws]
    in_specs += [
        pl.BlockSpec((1, d), const2),
        pl.BlockSpec((wq.shape[1], d), const2),
        pl.BlockSpec((M_DIM, 1), const2),
        pl.BlockSpec(k.shape, const2),
        pl.BlockSpec(v_t.shape, lambda i: (0, 0, 0)),
        pl.BlockSpec(wo.shape, const2),
    ]
    return pl.pallas_call(
        functools.partial(_mix_out_mem_body, n),
        grid=(s // tm,),
        in_specs=in_specs,
        out_specs=pl.BlockSpec((tm, d), lambda i: (i, 0)),
        out_shape=jax.ShapeDtypeStruct((s, d), F32),
        compiler_params=_cparams(1),
        name="mix_out_mem",
    )(h, *ots, *[w.astype(BF16) for w in ws], g.reshape(1, d), wq.T.astype(BF16),
      qg.astype(F32).reshape(M_DIM, 1), k, v_t, wo.astype(BF16))


def _softmax_step_ref(s_ref, stream, vts, m_ref, acc_ref):
    m_old = m_ref[stream]
    m_new = jnp.maximum(m_old, jnp.max(s_ref[stream], axis=0, keepdims=True))
    pv = None
    for u in range(KC):
        p = jnp.exp2(s_ref[stream, u * TK:(u + 1) * TK, :] - m_new)
        d = _dot(vts[u], p.astype(BF16))
        pv = d if pv is None else pv + d
    acc_ref[stream] = jnp.exp2(m_old - m_new) * acc_ref[stream] + pv
    m_ref[stream] = m_new


def _with_ones(vt):
    return jnp.concatenate([vt, jnp.ones((ROWSUM_ROWS, vt.shape[1]), vt.dtype)], axis=0)


def _normalised(acc, dv):
    return acc[:dv] / acc[dv:dv + 1]


def _score_pipeline(i, n_chunks, k_ref, q_halves, bias_ref, heads, mask_fn, consume,
                    s_even, s_odd):
    def produce(near, c, s_ref):
        for u in range(KC):
            j = c * KC + u
            k0 = pl.multiple_of(j * TK, TK)
            kt = k_ref[pl.ds(k0, TK), :]
            dist = i - j
            which = jnp.where(dist < 0, 3, jnp.minimum(dist, 2))
            sel = None if mask_fn is None else mask_fn(k0)
            for a, qh in enumerate(q_halves):
                s = _dot(kt, qh)
                if near:
                    s = s + bias_ref[heads[a], which]
                s_ref[a, u * TK:(u + 1) * TK, :] = s if sel is None else jnp.where(sel, s, NEG)

    def pair(near, p, carry):
        c = 2 * p
        produce(near, c + 1, s_odd)
        consume(c, s_even)
        produce(near, c + 2, s_even)
        consume(c + 1, s_odd)
        return carry

    n_far = jnp.maximum(i - 1, 0) // KC
    far_pairs = jnp.maximum(n_far - 1, 0) // 2
    full_pairs = (n_chunks - 1) // 2
    produce(True, 0, s_even)
    lax.fori_loop(0, far_pairs, functools.partial(pair, False), 0)
    lax.fori_loop(far_pairs, full_pairs, functools.partial(pair, True), 0)

    @pl.when(n_chunks % 2 == 1)
    def _last_one():
        consume(n_chunks - 1, s_even)

    @pl.when(n_chunks % 2 == 0)
    def _last_two():
        produce(True, n_chunks - 1, s_odd)
        consume(n_chunks - 2, s_even)
        consume(n_chunks - 1, s_odd)


def _init_state(m_ref, acc_ref):
    m_ref[...] = jnp.full(m_ref.shape, NEG, F32)
    acc_ref[...] = jnp.zeros_like(acc_ref)


def _half_masked(q):
    head = lax.broadcasted_iota(I32, q.shape, 0) // HEAD_DIM
    zero = jnp.zeros_like(q)
    return tuple(jnp.where(head == n, q, zero) for n in range(q.shape[0] // HEAD_DIM))


def _t5_bucket(rel):
    nb = T5_BUCKETS // 2
    max_exact = nb // 2
    offset = (rel < 0).astype(jnp.int32) * nb
    n = jnp.abs(rel)
    nf = jnp.maximum(n, 1).astype(jnp.float32)
    large = max_exact + (jnp.log(nf / max_exact) / math.log(T5_MAX_DIST / max_exact)
                         * (nb - max_exact)).astype(jnp.int32)
    large = jnp.minimum(large, nb - 1)
    return offset + jnp.where(n < max_exact, n, large)


def _toeplitz_body(masks, fills, n_keys, n_q, w_ref, o_ref):
    sk = lax.broadcasted_iota(I32, (n_keys, n_q), 0)
    tq = lax.broadcasted_iota(I32, (n_keys, n_q), 1)
    for blk, mask in enumerate(masks):
        w = jnp.broadcast_to(w_ref[0, blk], (n_keys, w_ref.shape[-1]))
        t = pltpu.roll(w, 0, 1, stride=1, stride_axis=0)[:, :n_q]
        o_ref[0, blk] = t if mask is None else jnp.where(mask(sk, tq), t, NEG)
    for n, fill in enumerate(fills):
        o_ref[0, len(masks) + n] = jnp.full((n_keys, n_q), fill, F32)


def _toeplitz_tiles(u, n_keys, n_q, masks, fills=()):
    hn, nb, _ = u.shape
    width = pl.cdiv(n_keys + n_q - 1, 128) * 128
    w = jnp.concatenate([u[..., n_keys - 1:],
                         jnp.zeros((hn, nb, width - (n_keys + n_q - 1)), F32),
                         u[..., :n_keys - 1]], axis=-1)
    n_out = nb + len(fills)
    return pl.pallas_call(
        functools.partial(_toeplitz_body, tuple(masks), tuple(fills), n_keys, n_q),
        grid=(hn,),
        in_specs=[pl.BlockSpec((1, nb, 1, width), lambda h: (h, 0, 0, 0))],
        out_specs=pl.BlockSpec((1, n_out, n_keys, n_q), lambda h: (h, 0, 0, 0)),
        out_shape=jax.ShapeDtypeStruct((hn, n_out, n_keys, n_q), F32),
        compiler_params=_cparams(1),
        name="toeplitz_tiles",
    )(w.reshape(hn, nb, 1, width))


def _t5_bias_tiles(table):
    table = table.astype(F32)
    far = table[_t5_bucket(jnp.full((1,), 2 * TK, jnp.int32))].T
    r = jnp.arange(-(TK - 1), TQ, dtype=jnp.int32)
    u = jnp.stack([(table[_t5_bucket(dist * TK + r)].T - far) * LOG2E for dist in range(2)],
                  axis=1)
    chunk_causal = lambda sk, tq: (sk // CHUNK) <= (tq // CHUNK)
    return _toeplitz_tiles(u, TK, TQ, (chunk_causal, None), fills=(0.0, NEG))


def _key_to_f32(key):
    return lax.bitcast_convert_type(jnp.where(key < 0, INT_MIN - key, key), F32)


def _score_codes(x):
    b = lax.bitcast_convert_type(x, I32)
    b = jnp.where((b & 0x7FFFFFFF) < MIN_NORMAL_BITS, 0, b)
    key = jnp.where(b < 0, INT_MIN - b, b)
    fb = jnp.where(b < 0, b + 0xFFFF, b) & jnp.int32(-65536)
    return key, lax.bitcast_convert_type(fb, F32).astype(BF16)


def _tree_sum(x):
    while x.shape[0] > 1:
        half = x.shape[0] // 2
        x = x[:half] + x[half:]
    return x[0]


def _select_topk(keys_ref, hb_ref, thr_ref, n_chunks, top_k, idx_bits):
    rows = KC * TK
    one, zero = jnp.ones((), BF16), jnp.zeros((), BF16)

    def count16(cand):
        cand = jnp.where((cand > 0) & (cand < MIN_NORMAL_BITS), MIN_NORMAL_BITS, cand)
        tb = _key_to_f32(cand).astype(BF16)

        def body(c, acc):
            c0 = pl.multiple_of(c * rows, rows)
            ge = jnp.where(hb_ref[pl.ds(c0, rows), :] >= tb, one, zero)
            return acc + _tree_sum(ge.reshape(rows // 16, 16, TQ)).astype(F32)

        acc = lax.fori_loop(0, n_chunks, body, jnp.zeros((16, TQ), F32))
        return jnp.sum(acc, axis=0, keepdims=True).astype(I32)

    def count32(pred):
        def body(c, acc):
            c0 = pl.multiple_of(c * rows, rows)
            hit = pred(keys_ref[pl.ds(c0, rows), :], c0).astype(I32)
            return acc + jnp.sum(hit.reshape(rows // 8, 8, TQ), axis=0)

        acc = lax.fori_loop(0, n_chunks, body, jnp.zeros((8, TQ), I32))
        return jnp.sum(acc, axis=0, keepdims=True)

    def search(count, bits, thr, c_thr, n_fixed):
        def step(carry):
            b, thr, c_thr = carry
            cand = thr + lax.shift_left(jnp.int32(1), bits[0] - b)
            cnt = count(cand)
            ok = cnt >= top_k
            return b + 1, jnp.where(ok, cand, thr), jnp.where(ok, cnt, c_thr)

        def unresolved(carry):
            b, _, c_thr = carry
            return (b <= bits[0] - bits[1]) & (jnp.max(c_thr) > top_k)

        carry = lax.fori_loop(0, n_fixed, lambda _, c: step(c), (jnp.int32(0), thr, c_thr))
        return lax.while_loop(unresolved, step, carry)[1:]

    cnt = count16(jnp.zeros((1, TQ), I32))
    ok = cnt >= top_k
    thr = jnp.where(ok, 0, INT_MIN).astype(I32)
    c_thr = jnp.where(ok, cnt, 2 ** 30)
    thr, c_thr = search(count16, (30, 16), thr, c_thr, 15)
    thr, c_thr = search(lambda cand: count32(lambda keys, c0: keys >= cand), (15, 0), thr, c_thr, 11)
    thr_ref[...] = jnp.maximum(thr, INT_MIN + 1)

    split = (c_thr > top_k) & (thr > INT_MIN)

    @pl.when(jnp.max(split.astype(I32)) > 0)
    def _break_ties():
        need = top_k - count32(lambda keys, c0: keys > thr)

        def equal_before(limit):
            def pred(keys, c0):
                idx = c0 + lax.broadcasted_iota(I32, keys.shape, 0)
                return (keys == thr) & (idx < limit)
            return count32(pred)

        def step(b, q):
            cand = q + lax.shift_left(jnp.int32(1), idx_bits - 1 - b)
            return jnp.where(equal_before(cand) < need, cand, q)

        q = lax.fori_loop(0, idx_bits, step, jnp.zeros((1, TQ), I32))

        def lower(c, carry):
            c0 = pl.multiple_of(c * rows, rows)
            keys = keys_ref[pl.ds(c0, rows), :]
            idx = c0 + lax.broadcasted_iota(I32, keys.shape, 0)
            drop = split & (keys == thr) & (idx > q)
            keys_ref[pl.ds(c0, rows), :] = jnp.where(drop, thr - 1, keys)
            return carry

        lax.fori_loop(0, n_chunks, lower, 0)


def _dsa_body(top_k, idx_bits, iq_ref, iw_ref, ik_ref, q_ref, k_ref, vt_ref, bias_ref, o_ref,
              keys_ref, hb_ref, thr_ref, m_ref, acc_ref, s_even, s_odd):
    i = pl.program_id(0)
    g = pl.program_id(1)
    n_tiles = i + 1
    n_chunks = (i + KC) // KC

    @pl.when(g == 0)
    def _select():
        w_all = iw_ref[...]
        zpad = jnp.zeros((IDX_DIM, TQ), BF16)
        qz = [jnp.concatenate([iq_ref[h * IDX_DIM:(h + 1) * IDX_DIM, :], zpad], axis=0)
              for h in range(IDX_HEADS)]

        def score_tiles(p, carry):
            for u in range(2):
                k0 = pl.multiple_of((2 * p + u) * TK, TK)
                ikt = ik_ref[pl.ds(k0, TK), :]
                sc = jnp.zeros((TK, TQ), F32)
                for h in range(IDX_HEADS):
                    sc = sc + w_all[h:h + 1, :] * jnp.maximum(_dot(ikt, qz[h]), 0.0)
                keys_ref[pl.ds(k0, TK), :], hb_ref[pl.ds(k0, TK), :] = _score_codes(sc)
            return carry

        lax.fori_loop(0, (n_tiles + 1) // 2, score_tiles, 0)

        d0 = pl.multiple_of(i * TK, TK)
        sk = lax.broadcasted_iota(I32, (TK, TQ), 0) // CHUNK
        tq = lax.broadcasted_iota(I32, (TK, TQ), 1) // CHUNK
        no_key = jnp.full((TK, TQ), INT_MIN, I32)
        no_hb = jnp.full((TK, TQ), jnp.nan, BF16)
        keys_ref[pl.ds(d0, TK), :] = jnp.where(sk <= tq, keys_ref[pl.ds(d0, TK), :], no_key)
        hb_ref[pl.ds(d0, TK), :] = jnp.where(sk <= tq, hb_ref[pl.ds(d0, TK), :], no_hb)

        def fill_tile(j, carry):
            k0 = pl.multiple_of(j * TK, TK)
            keys_ref[pl.ds(k0, TK), :] = no_key
            hb_ref[pl.ds(k0, TK), :] = no_hb
            return carry

        lax.fori_loop(n_tiles, n_chunks * KC, fill_tile, 0)
        _select_topk(keys_ref, hb_ref, thr_ref, n_chunks, top_k, idx_bits)

    _init_state(m_ref, acc_ref)
    q_halves = _half_masked(q_ref[...])
    thr = thr_ref[...]

    def selected(k0):
        return keys_ref[pl.ds(k0, TK), :] >= thr

    def consume(c, s_ref):
        for a in range(2):
            vts = [_with_ones(vt_ref[c * KC + u, a * HEAD_DIM:(a + 1) * HEAD_DIM, :])
                   for u in range(KC)]
            _softmax_step_ref(s_ref, a, vts, m_ref, acc_ref)

    _score_pipeline(i, n_chunks, k_ref, q_halves, bias_ref, (0, 1), selected, consume,
                    s_even, s_odd)
    for a in range(2):
        o_ref[a * HEAD_DIM:(a + 1) * HEAD_DIM, :] = _normalised(acc_ref[a], HEAD_DIM)


def _dsa_attention(iq_t, iw_t, ik, q_t, k, v_t, bias, top_k):
    s = k.shape[0]
    n_pairs = A_HEADS // 2
    return pl.pallas_call(
        functools.partial(_dsa_body, top_k, (s - 1).bit_length()),
        grid=(s // TQ, n_pairs),
        in_specs=[
            pl.BlockSpec((IDX_HEADS * IDX_DIM, TQ), lambda i, g: (0, i)),
            pl.BlockSpec((IDX_HEADS, TQ), lambda i, g: (0, i)),
            pl.BlockSpec((s, 128), lambda i, g: (0, 0)),
            pl.BlockSpec((128, TQ), lambda i, g: (g, i)),
            pl.BlockSpec((s, 128), lambda i, g: (0, g)),
            pl.BlockSpec((s // TK, 128, TK), lambda i, g: (0, g, 0)),
            pl.BlockSpec((2, 4, TK, TQ), lambda i, g: (g, 0, 0, 0)),
        ],
        out_specs=pl.BlockSpec((128, TQ), lambda i, g: (g, i)),
        out_shape=jax.ShapeDtypeStruct((A_HEADS * HEAD_DIM, s), F32),
        scratch_shapes=[
            pltpu.VMEM((s, TQ), I32),
            pltpu.VMEM((s, TQ), BF16),
            pltpu.VMEM((1, TQ), I32),
            pltpu.VMEM((2, 1, TQ), F32),
            pltpu.VMEM((2, HEAD_DIM + ROWSUM_ROWS, TQ), F32),
            pltpu.VMEM((2, KC * TK, TQ), F32),
            pltpu.VMEM((2, KC * TK, TQ), F32),
        ],
        compiler_params=_cparams(2),
        name="dsa_attention",
    )(iq_t, iw_t, ik, q_t, k, v_t, bias)


def _diff_body(lambda_init, q_ref, k_ref, vt_ref, bias_ref, lam_ref, subln_ref, o_ref,
               m_ref, acc_ref, s_even, s_odd):
    i = pl.program_id(0)
    _init_state(m_ref, acc_ref)
    q_halves = _half_masked(q_ref[...])

    def consume(c, s_ref):
        for hd in range(DIFF_HEADS):
            rows = slice(hd * B_VDIM, (hd + 1) * B_VDIM)
            vts = [_with_ones(vt_ref[c * KC + u, rows, :]) for u in range(KC)]
            for m in range(2):
                _softmax_step_ref(s_ref, 2 * hd + m, vts, m_ref, acc_ref)

    heads = tuple(hd for hd in range(DIFF_HEADS) for _ in range(2))
    _score_pipeline(i, (i + KC) // KC, k_ref, q_halves, bias_ref, heads, None, consume,
                    s_even, s_odd)

    lq1, lk1, lq2, lk2 = (lam_ref[r:r + 1, :] for r in range(4))
    lam = (jnp.exp(jnp.sum(lq1 * lk1, keepdims=True)) - jnp.exp(jnp.sum(lq2 * lk2, keepdims=True))
           + lambda_init)
    for hd in range(DIFF_HEADS):
        o = (_normalised(acc_ref[2 * hd], B_VDIM)
             - lam * _normalised(acc_ref[2 * hd + 1], B_VDIM))
        ms = jnp.mean(o * o, axis=0, keepdims=True)
        o_ref[hd * B_VDIM:(hd + 1) * B_VDIM, :] = (o * lax.rsqrt(ms + EPS) * subln_ref[...]
                                                   * (1.0 - lambda_init))


def _diff_attention(q_t, k, v_t, bias, lam_rows, subln, lambda_init):
    s = k.shape[0]
    f = DIFF_HEADS * B_VDIM
    n_streams = 2 * DIFF_HEADS
    return pl.pallas_call(
        functools.partial(_diff_body, lambda_init),
        grid=(s // TQ, B_HEADS // DIFF_HEADS),
        in_specs=[
            pl.BlockSpec((f, TQ), lambda i, h: (h, i)),
            pl.BlockSpec((s, f), lambda i, h: (0, h)),
            pl.BlockSpec((s // TK, f, TK), lambda i, h: (0, h, 0)),
            pl.BlockSpec((DIFF_HEADS, 4, TK, TQ), lambda i, h: (h, 0, 0, 0)),
            pl.BlockSpec((4, HEAD_DIM), lambda i, h: (0, 0)),
            pl.BlockSpec((B_VDIM, 1), lambda i, h: (0, 0)),
        ],
        out_specs=pl.BlockSpec((f, TQ), lambda i, h: (h, i)),
        out_shape=jax.ShapeDtypeStruct((B_HEADS * B_VDIM, s), F32),
        scratch_shapes=[
            pltpu.VMEM((n_streams, 1, TQ), F32),
            pltpu.VMEM((n_streams, B_VDIM + ROWSUM_ROWS, TQ), F32),
            pltpu.VMEM((n_streams, KC * TK, TQ), F32),
            pltpu.VMEM((n_streams, KC * TK, TQ), F32),
        ],
        compiler_params=_cparams(2),
        name="diff_attention",
    )(q_t, k, v_t, bias, lam_rows, subln.astype(F32).reshape(B_VDIM, 1))


def _band_bias_tiles(rel_bias):
    table = rel_bias.astype(F32) * LOG2E
    r = jnp.arange(-(BAND_T - 1), BAND_T, dtype=jnp.int32)
    u = jnp.stack([table[jnp.clip((1 - blk) * BAND_T + r, -REL_CLIP, REL_CLIP) + REL_CLIP].T
                   for blk in range(2)], axis=1)

    def in_band(blk):
        def mask(sk, tq):
            kc = sk // CHUNK + blk * (BAND_T // CHUNK)
            qc = tq // CHUNK + BAND_T // CHUNK
            return (kc <= qc) & (kc >= qc - (C_BAND - 1))
        return mask

    return _toeplitz_tiles(u, BAND_T, BAND_T, (in_band(0), in_band(1)))


def _band_body(q_ref, kp_ref, kc_ref, vp_ref, vc_ref, bias_ref, o_ref):
    i = pl.program_id(1)
    q_halves = _half_masked(q_ref[...])
    for a in range(len(q_halves)):
        rows = slice(a * HEAD_DIM, (a + 1) * HEAD_DIM)
        s_prev = _dot(kp_ref[...], q_halves[a]) + jnp.where(i > 0, bias_ref[a, 0], NEG)
        s_cur = _dot(kc_ref[...], q_halves[a]) + bias_ref[a, 1]
        m = jnp.maximum(jnp.max(s_prev, axis=0, keepdims=True),
                        jnp.max(s_cur, axis=0, keepdims=True))
        acc = (_dot(_with_ones(vp_ref[0, rows, :]), jnp.exp2(s_prev - m).astype(BF16))
               + _dot(_with_ones(vc_ref[0, rows, :]), jnp.exp2(s_cur - m).astype(BF16)))
        o_ref[rows, :] = _normalised(acc, HEAD_DIM)


def _band_attention(q_t, k, v_t, bias):
    s = k.shape[0]
    t = BAND_T
    f = BAND_HEADS * HEAD_DIM
    prev = lambda i: jnp.maximum(i - 1, 0)
    return pl.pallas_call(
        _band_body,
        grid=(C_HEADS // BAND_HEADS, s // t),
        in_specs=[
            pl.BlockSpec((f, t), lambda p, i: (p, i)),
            pl.BlockSpec((t, f), lambda p, i: (prev(i), p)),
            pl.BlockSpec((t, f), lambda p, i: (i, p)),
            pl.BlockSpec((1, f, t), lambda p, i: (prev(i), p, 0)),
            pl.BlockSpec((1, f, t), lambda p, i: (i, p, 0)),
            pl.BlockSpec((BAND_HEADS, 2, t, t), lambda p, i: (p, 0, 0, 0)),
        ],
        out_specs=pl.BlockSpec((f, t), lambda p, i: (p, i)),
        out_shape=jax.ShapeDtypeStruct((C_HEADS * HEAD_DIM, s), F32),
        compiler_params=_cparams(2),
        name="band_attention",
    )(q_t, k, k, v_t, v_t, bias)


def _even_mixer(h, mix_g, w_in, a_qg, a_kg, idx_kg, b_qg, b_kg, lq1, lk1, lq2, lk2, b_subln,
                w_out, t5_bias, layer_idx):
    s = h.shape[0]
    sizes = [A_HEADS * HEAD_DIM] * 3 + [IDX_HEADS * IDX_DIM, IDX_DIM, IDX_HEADS] + \
            [B_HEADS * 2 * HEAD_DIM] * 2 + [B_HEADS * B_VDIM]
    cuts = np.cumsum([0] + sizes)
    w = [w_in[:, cuts[n]:cuts[n + 1]] for n in range(len(sizes))]
    qk_scale = HEAD_DIM ** -0.5 * LOG2E
    segs = [
        _Seg(w[0], HEAD_DIM, a_qg, qk_scale, "fm"),
        _Seg(w[1], HEAD_DIM, a_kg, 1.0, "tm"),
        _Seg(w[2], layout="vt"),
        _Seg(w[3], scale=IDX_DIM ** -0.5, layout="fm"),
        _Seg(w[4], IDX_DIM, idx_kg, 1.0, "tm", pad=IDX_DIM),
        _Seg(w[5], scale=IDX_HEADS ** -0.5, layout="fm", dtype=F32),
        _Seg(w[6], HEAD_DIM, b_qg, qk_scale, "fm"),
        _Seg(w[7], HEAD_DIM, b_kg, 1.0, "tm"),
        _Seg(w[8], layout="vt"),
    ]
    aq_t, ak, av_t, iq_t, ik, iw_t, bq_t, bk, bv_t = _project(h, mix_g, segs)
    bias = _t5_bias_tiles(t5_bias)
    assert s % (KC * TK) == 0
    top_k = min(TOPK_MAX, s // 4)
    out_a = _dsa_attention(iq_t, iw_t, ik, aq_t, ak, av_t, bias[:A_HEADS], top_k)
    lambda_init = 0.8 - 0.6 * math.exp(-0.3 * layer_idx)
    lam_rows = jnp.stack([lq1, lk1, lq2, lk2]).astype(F32)
    out_b = _diff_attention(bq_t, bk, bv_t, bias[A_HEADS:], lam_rows, b_subln, lambda_init)
    na = A_HEADS * HEAD_DIM
    return [out_a, out_b], [w_out[:na], w_out[na:]]


def _odd_mixer(h, mix_g, w_in, c_qg, c_kg, rel_bias, w_out):
    f = C_HEADS * HEAD_DIM
    segs = [
        _Seg(w_in[:, :f], HEAD_DIM, c_qg, HEAD_DIM ** -0.5 * LOG2E, "fm"),
        _Seg(w_in[:, f:2 * f], HEAD_DIM, c_kg, 1.0, "tm"),
        _Seg(w_in[:, 2 * f:], layout="vt", tile=BAND_T),
    ]
    q_t, k, v_t = _project(h, mix_g, segs)
    out = _band_attention(q_t, k, v_t, _band_bias_tiles(rel_bias))
    return [out], [w_out]


def _mixer_out_and_memory(h, mixer_out, mem, mg, sg, wq, wkv, qg, kg, wo):
    ots, ws = mixer_out
    f = M_HEADS * M_DIM
    k, v_t = _project(mem, sg, [_Seg(wkv[:, :f], M_DIM, kg, 1.0, "tm"),
                                _Seg(wkv[:, f:], layout="vt", tile=mem.shape[0])])
    return _mix_out_mem(h, ots, ws, mg, wq, qg, k, v_t, wo)


def kernel(x, mem, t5_bias,
           l0_ffn1_norm, l0_ffn1_wg, l0_ffn1_wu, l0_ffn1_wd,
           l0_mix_norm, l0_w_in, l0_a_q_norm, l0_a_k_norm, l0_idx_k_norm,
           l0_b_q_norm, l0_b_k_norm, l0_b_lq1, l0_b_lk1, l0_b_lq2, l0_b_lk2, l0_b_subln, l0_w_out,
           l0_mem_norm, l0_mem_src_norm, l0_mem_wq, l0_mem_wkv, l0_mem_q_norm, l0_mem_k_norm, l0_mem_wo,
           l0_ffn2_norm, l0_ffn2_wg, l0_ffn2_wu, l0_ffn2_wd,
           l1_ffn1_norm, l1_ffn1_wg, l1_ffn1_wu, l1_ffn1_wd,
           l1_mix_norm, l1_w_in, l1_c_q_norm, l1_c_k_norm, l1_c_rel_bias, l1_w_out,
           l1_mem_norm, l1_mem_src_norm, l1_mem_wq, l1_mem_wkv, l1_mem_q_norm, l1_mem_k_norm, l1_mem_wo,
           l1_ffn2_norm, l1_ffn2_wg, l1_ffn2_wu, l1_ffn2_wd):
    bsz, seq, d = x.shape
    assert bsz == 1 and mem.shape[0] == 1
    h = x.reshape(seq, d)
    m = mem.reshape(mem.shape[1], d)

    h = _ffn(h, l0_ffn1_norm, l0_ffn1_wg, l0_ffn1_wu, l0_ffn1_wd)
    mixed = _even_mixer(h, l0_mix_norm, l0_w_in, l0_a_q_norm, l0_a_k_norm, l0_idx_k_norm,
                        l0_b_q_norm, l0_b_k_norm, l0_b_lq1, l0_b_lk1, l0_b_lq2, l0_b_lk2,
                        l0_b_subln, l0_w_out, t5_bias, 0)
    h = _mixer_out_and_memory(h, mixed, m, l0_mem_norm, l0_mem_src_norm, l0_mem_wq, l0_mem_wkv,
                              l0_mem_q_norm, l0_mem_k_norm, l0_mem_wo)
    h = _ffn(h, l0_ffn2_norm, l0_ffn2_wg, l0_ffn2_wu, l0_ffn2_wd)

    h = _ffn(h, l1_ffn1_norm, l1_ffn1_wg, l1_ffn1_wu, l1_ffn1_wd)
    mixed = _odd_mixer(h, l1_mix_norm, l1_w_in, l1_c_q_norm, l1_c_k_norm, l1_c_rel_bias, l1_w_out)
    h = _mixer_out_and_memory(h, mixed, m, l1_mem_norm, l1_mem_src_norm, l1_mem_wq, l1_mem_wkv,
                              l1_mem_q_norm, l1_mem_k_norm, l1_mem_wo)
    h = _ffn(h, l1_ffn2_norm, l1_ffn2_wg, l1_ffn2_wu, l1_ffn2_wd)
    return h.reshape(bsz, seq, d)
```

```python
import functools
import math

import jax
import jax.numpy as jnp
import numpy as np
from jax import lax
from jax.experimental import pallas as pl
from jax.experimental.pallas import tpu as pltpu

D_MODEL = 1024
CHUNK = 64
HEAD_DIM = 64
EPS = 1e-6
A_HEADS = 8
IDX_HEADS = 8
IDX_DIM = 64
TOPK_MAX = 256
B_VDIM = 128
B_HEADS = 4
C_HEADS = 16
C_BAND = 9
REL_CLIP = 256
T5_BUCKETS = 32
T5_MAX_DIST = 128
M_HEADS = 4
M_DIM = 128
D_FF = 2816

F32 = jnp.float32
BF16 = jnp.bfloat16
I32 = jnp.int32

NEG = -1e30
LOG2E = math.log2(math.e)
INT_MIN = -(2 ** 31)
MIN_NORMAL_BITS = 0x00800000
VMEM_LIMIT = 60 * 1024 * 1024

ROW_TILE = 512
FF_CHUNK = 256
TQ = 256
TK = 256
KC = 4
ROWSUM_ROWS = 16
BAND_T = 512
BAND_HEADS = 4
DIFF_HEADS = 1


def _cparams(n_axes):
    return pltpu.CompilerParams(dimension_semantics=("arbitrary",) * n_axes,
                                vmem_limit_bytes=VMEM_LIMIT)


def _dot(a, b):
    return jnp.dot(a, b, preferred_element_type=F32)


def _rms_rows(x, g):
    ms = jnp.mean(x * x, axis=-1, keepdims=True)
    return x * lax.rsqrt(ms + EPS) * g


def _ffn_body(h_ref, g_ref, wg_ref, wu_ref, wd_ref, o_ref, xn_ref, acc_ref):
    x = h_ref[...]
    xn_ref[...] = _rms_rows(x, g_ref[...]).astype(BF16)
    acc_ref[...] = jnp.zeros_like(acc_ref)

    def chunk(c, carry):
        xn = xn_ref[...]
        cols = pl.ds(pl.multiple_of(c * FF_CHUNK, FF_CHUNK), FF_CHUNK)
        gate = _dot(xn, wg_ref[:, cols])
        up = _dot(xn, wu_ref[:, cols])
        act = (gate * jax.nn.sigmoid(gate) * up).astype(BF16)
        acc_ref[...] += _dot(act, wd_ref[c])
        return carry

    lax.fori_loop(0, wd_ref.shape[0], chunk, 0)
    o_ref[...] = x + 0.5 * acc_ref[...]


def _ffn(h, g, wg, wu, wd):
    s, d = h.shape
    ff = wg.shape[1]
    nc = ff // FF_CHUNK
    wd3 = wd.astype(BF16).reshape(nc, FF_CHUNK, d)
    tm = min(ROW_TILE, s)
    const2 = lambda i: (0, 0)
    return pl.pallas_call(
        _ffn_body,
        grid=(s // tm,),
        in_specs=[
            pl.BlockSpec((tm, d), lambda i: (i, 0)),
            pl.BlockSpec((1, d), lambda i: (0, 0)),
            pl.BlockSpec((d, ff), const2, pipeline_mode=pl.Buffered(1)),
            pl.BlockSpec((d, ff), const2, pipeline_mode=pl.Buffered(1)),
            pl.BlockSpec((nc, FF_CHUNK, d), lambda i: (0, 0, 0), pipeline_mode=pl.Buffered(1)),
        ],
        out_specs=pl.BlockSpec((tm, d), lambda i: (i, 0)),
        out_shape=jax.ShapeDtypeStruct((s, d), F32),
        scratch_shapes=[pltpu.VMEM((tm, d), BF16), pltpu.VMEM((tm, d), F32)],
        compiler_params=_cparams(1),
        name="ffn",
    )(h, g.reshape(1, d), wg.astype(BF16), wu.astype(BF16), wd3)


class _Seg:
    def __init__(self, w, head_dim=None, gain=None, scale=1.0, layout="fm",
                 dtype=BF16, tile=TK, pad=0):
        self.w = w
        self.head_dim = head_dim
        self.gain = gain
        self.scale = scale
        self.layout = layout
        self.dtype = dtype
        self.tile = tile
        self.pad = pad


def _proj_body(segs, x_ref, g_ref, *refs):
    n = len(segs)
    w_refs = refs[:n]
    gains = [r for r in refs[n:2 * n]]
    outs = refs[2 * n:]
    xn = _rms_rows(x_ref[...], g_ref[...]).astype(BF16)
    tm = xn.shape[0]
    for seg, w_ref, gain_ref, o_ref in zip(segs, w_refs, gains, outs):
        yt = lax.dot_general(w_ref[...], xn, (((0,), (1,)), ((), ())),
                             preferred_element_type=F32)
        f = yt.shape[0]
        if seg.head_dim is not None:
            hd = seg.head_dim
            y3 = yt.reshape(f // hd, hd, tm)
            ms = jnp.mean(y3 * y3, axis=1, keepdims=True)
            y3 = y3 * lax.rsqrt(ms + EPS) * gain_ref[...][None]
            yt = y3.reshape(f, tm)
        if seg.scale != 1.0:
            yt = yt * seg.scale
        if seg.pad:
            yt = jnp.concatenate([yt, jnp.zeros((seg.pad, tm), F32)], axis=0)
        if seg.layout == "fm":
            o_ref[...] = yt.astype(seg.dtype)
        elif seg.layout == "tm":
            o_ref[...] = yt.T.astype(seg.dtype)
        else:
            for t in range(tm // seg.tile):
                o_ref[t] = yt[:, t * seg.tile:(t + 1) * seg.tile].astype(seg.dtype)


def _project(x, g, segs):
    s, d = x.shape
    tm = min(ROW_TILE, s)
    in_specs = [pl.BlockSpec((tm, d), lambda i: (i, 0)),
                pl.BlockSpec((1, d), lambda i: (0, 0))]
    args = [x, g.reshape(1, d)]
    for seg in segs:
        wt = seg.w.astype(BF16)
        args.append(wt)
        in_specs.append(pl.BlockSpec(wt.shape, lambda i: (0, 0)))
    for seg in segs:
        hd = seg.head_dim or 8
        gain = seg.gain if seg.gain is not None else jnp.ones((hd,), F32)
        args.append(gain.astype(F32).reshape(hd, 1))
        in_specs.append(pl.BlockSpec((hd, 1), lambda i: (0, 0)))
    out_shapes, out_specs = [], []
    for seg in segs:
        f = seg.w.shape[1] + seg.pad
        if seg.layout == "fm":
            out_shapes.append(jax.ShapeDtypeStruct((f, s), seg.dtype))
            out_specs.append(pl.BlockSpec((f, tm), lambda i: (0, i)))
        elif seg.layout == "tm":
            out_shapes.append(jax.ShapeDtypeStruct((s, f), seg.dtype))
            out_specs.append(pl.BlockSpec((tm, f), lambda i: (i, 0)))
        else:
            nt = tm // seg.tile
            out_shapes.append(jax.ShapeDtypeStruct((s // seg.tile, f, seg.tile), seg.dtype))
            out_specs.append(pl.BlockSpec((nt, f, seg.tile), lambda i: (i, 0, 0)))
    return pl.pallas_call(
        functools.partial(_proj_body, segs),
        grid=(s // tm,),
        in_specs=in_specs,
        out_specs=out_specs,
        out_shape=out_shapes,
        compiler_params=_cparams(1),
        name="project",
    )(*args)


def _mix_out_mem_body(n, h_ref, *refs):
    ot_refs, w_refs = refs[:n], refs[n:2 * n]
    g_ref, wq_ref, qg_ref, k_ref, vt_ref, wo_ref, o_ref = refs[2 * n:]
    h1 = h_ref[...]
    for ot_ref, w_ref in zip(ot_refs, w_refs):
        h1 = h1 + _dot(ot_ref[...].T.astype(BF16), w_ref[...])
    tm = h1.shape[0]

    xn = _rms_rows(h1, g_ref[...]).astype(BF16)
    qt = lax.dot_general(wq_ref[...], xn, (((1,), (1,)), ((), ())), preferred_element_type=F32)
    q3 = qt.reshape(M_HEADS, M_DIM, tm)
    ms = jnp.mean(q3 * q3, axis=1, keepdims=True)
    q3 = q3 * lax.rsqrt(ms + EPS) * qg_ref[...][None] * (M_DIM ** -0.5 * LOG2E)
    q = q3.reshape(M_HEADS * M_DIM, tm).astype(BF16)

    outs = []
    for hd in range(M_HEADS):
        rows = slice(hd * M_DIM, (hd + 1) * M_DIM)
        s = _dot(k_ref[:, rows], q[rows, :])
        p = jnp.exp2(s - jnp.max(s, axis=0, keepdims=True))
        outs.append(_normalised(_dot(_with_ones(vt_ref[0, rows, :]), p.astype(BF16)), M_DIM))
    o = jnp.concatenate(outs, axis=0)
    o_ref[...] = h1 + _dot(o.T.astype(BF16), wo_ref[...])


def _mix_out_mem(h, ots, ws, g, wq, qg, k, v_t, wo):
    s, d = h.shape
    tm = min(ROW_TILE, s)
    n = len(ots)
    const2 = lambda i: (0, 0)
    in_specs = [pl.BlockSpec((tm, d), lambda i: (i, 0))]
    in_specs += [pl.BlockSpec((ot.shape[0], tm), lambda i: (0, i)) for ot in ots]
    in_specs += [pl.BlockSpec(w.shape, const2) for w in ws]
    in_specs += [
        pl.BlockSpec((1, d), const2),
        pl.BlockSpec((wq.shape[1], d), const2),
        pl.BlockSpec((M_DIM, 1), const2),
        pl.BlockSpec(k.shape, const2),
        pl.BlockSpec(v_t.shape, lambda i: (0, 0, 0)),
        pl.BlockSpec(wo.shape, const2),
    ]
    return pl.pallas_call(
        functools.partial(_mix_out_mem_body, n),
        grid=(s // tm,),
        in_specs=in_specs,
        out_specs=pl.BlockSpec((tm, d), lambda i: (i, 0)),
        out_shape=jax.ShapeDtypeStruct((s, d), F32),
        compiler_params=_cparams(1),
        name="mix_out_mem",
    )(h, *ots, *[w.astype(BF16) for w in ws], g.reshape(1, d), wq.T.astype(BF16),
      qg.astype(F32).reshape(M_DIM, 1), k, v_t, wo.astype(BF16))


def _softmax_step_ref(s_ref, stream, vts, m_ref, acc_ref):
    m_old = m_ref[stream]
    m_new = jnp.maximum(m_old, jnp.max(s_ref[stream], axis=0, keepdims=True))
    pv = None
    for u in range(KC):
        p = jnp.exp2(s_ref[stream, u * TK:(u + 1) * TK, :] - m_new)
        d = _dot(vts[u], p.astype(BF16))
        pv = d if pv is None else pv + d
    acc_ref[stream] = jnp.exp2(m_old - m_new) * acc_ref[stream] + pv
    m_ref[stream] = m_new


def _with_ones(vt):
    return jnp.concatenate([vt, jnp.ones((ROWSUM_ROWS, vt.shape[1]), vt.dtype)], axis=0)


def _normalised(acc, dv):
    return acc[:dv] / acc[dv:dv + 1]


def _score_pipeline(i, n_chunks, k_ref, q_halves, bias_ref, heads, mask_fn, consume,
                    s_even, s_odd):
    def produce(near, c, s_ref):
        for u in range(KC):
            j = c * KC + u
            k0 = pl.multiple_of(j * TK, TK)
            kt = k_ref[pl.ds(k0, TK), :]
            dist = i - j
            which = jnp.where(dist < 0, 3, jnp.minimum(dist, 2))
            sel = None if mask_fn is None else mask_fn(k0)
            for a, qh in enumerate(q_halves):
                s = _dot(kt, qh)
                if near:
                    s = s + bias_ref[heads[a], which]
                s_ref[a, u * TK:(u + 1) * TK, :] = s if sel is None else jnp.where(sel, s, NEG)

    def pair(near, p, carry):
        c = 2 * p
        produce(near, c + 1, s_odd)
        consume(c, s_even)
        produce(near, c + 2, s_even)
        consume(c + 1, s_odd)
        return carry

    n_far = jnp.maximum(i - 1, 0) // KC
    far_pairs = jnp.maximum(n_far - 1, 0) // 2
    full_pairs = (n_chunks - 1) // 2
    produce(True, 0, s_even)
    lax.fori_loop(0, far_pairs, functools.partial(pair, False), 0)
    lax.fori_loop(far_pairs, full_pairs, functools.partial(pair, True), 0)

    @pl.when(n_chunks % 2 == 1)
    def _last_one():
        consume(n_chunks - 1, s_even)

    @pl.when(n_chunks % 2 == 0)
    def _last_two():
        produce(True, n_chunks - 1, s_odd)
        consume(n_chunks - 2, s_even)
        consume(n_chunks - 1, s_odd)


def _init_state(m_ref, acc_ref):
    m_ref[...] = jnp.full(m_ref.shape, NEG, F32)
    acc_ref[...] = jnp.zeros_like(acc_ref)


def _half_masked(q):
    head = lax.broadcasted_iota(I32, q.shape, 0) // HEAD_DIM
    zero = jnp.zeros_like(q)
    return tuple(jnp.where(head == n, q, zero) for n in range(q.shape[0] // HEAD_DIM))


def _t5_bucket(rel):
    nb = T5_BUCKETS // 2
    max_exact = nb // 2
    offset = (rel < 0).astype(jnp.int32) * nb
    n = jnp.abs(rel)
    nf = jnp.maximum(n, 1).astype(jnp.float32)
    large = max_exact + (jnp.log(nf / max_exact) / math.log(T5_MAX_DIST / max_exact)
                         * (nb - max_exact)).astype(jnp.int32)
    large = jnp.minimum(large, nb - 1)
    return offset + jnp.where(n < max_exact, n, large)


def _toeplitz_body(masks, fills, n_keys, n_q, w_ref, o_ref):
    sk = lax.broadcasted_iota(I32, (n_keys, n_q), 0)
    tq = lax.broadcasted_iota(I32, (n_keys, n_q), 1)
    for blk, mask in enumerate(masks):
        w = jnp.broadcast_to(w_ref[0, blk], (n_keys, w_ref.shape[-1]))
        t = pltpu.roll(w, 0, 1, stride=1, stride_axis=0)[:, :n_q]
        o_ref[0, blk] = t if mask is None else jnp.where(mask(sk, tq), t, NEG)
    for n, fill in enumerate(fills):
        o_ref[0, len(masks) + n] = jnp.full((n_keys, n_q), fill, F32)


def _toeplitz_tiles(u, n_keys, n_q, masks, fills=()):
    hn, nb, _ = u.shape
    width = pl.cdiv(n_keys + n_q - 1, 128) * 128
    w = jnp.concatenate([u[..., n_keys - 1:],
                         jnp.zeros((hn, nb, width - (n_keys + n_q - 1)), F32),
                         u[..., :n_keys - 1]], axis=-1)
    n_out = nb + len(fills)
    return pl.pallas_call(
        functools.partial(_toeplitz_body, tuple(masks), tuple(fills), n_keys, n_q),
        grid=(hn,),
        in_specs=[pl.BlockSpec((1, nb, 1, width), lambda h: (h, 0, 0, 0))],
        out_specs=pl.BlockSpec((1, n_out, n_keys, n_q), lambda h: (h, 0, 0, 0)),
        out_shape=jax.ShapeDtypeStruct((hn, n_out, n_keys, n_q), F32),
        compiler_params=_cparams(1),
        name="toeplitz_tiles",
    )(w.reshape(hn, nb, 1, width))


def _t5_bias_tiles(table):
    table = table.astype(F32)
    far = table[_t5_bucket(jnp.full((1,), 2 * TK, jnp.int32))].T
    r = jnp.arange(-(TK - 1), TQ, dtype=jnp.int32)
    u = jnp.stack([(table[_t5_bucket(dist * TK + r)].T - far) * LOG2E for dist in range(2)],
                  axis=1)
    chunk_causal = lambda sk, tq: (sk // CHUNK) <= (tq // CHUNK)
    return _toeplitz_tiles(u, TK, TQ, (chunk_causal, None), fills=(0.0, NEG))


def _key_to_f32(key):
    return lax.bitcast_convert_type(jnp.where(key < 0, INT_MIN - key, key), F32)


def _score_codes(x):
    b = lax.bitcast_convert_type(x, I32)
    b = jnp.where((b & 0x7FFFFFFF) < MIN_NORMAL_BITS, 0, b)
    key = jnp.where(b < 0, INT_MIN - b, b)
    fb = jnp.where(b < 0, b + 0xFFFF, b) & jnp.int32(-65536)
    return key, lax.bitcast_convert_type(fb, F32).astype(BF16)


def _tree_sum(x):
    while x.shape[0] > 1:
        half = x.shape[0] // 2
        x = x[:half] + x[half:]
    return x[0]


def _select_topk(keys_ref, hb_ref, thr_ref, n_chunks, top_k, idx_bits):
    rows = KC * TK
    one, zero = jnp.ones((), BF16), jnp.zeros((), BF16)

    def count16(cand):
        cand = jnp.where((cand > 0) & (cand < MIN_NORMAL_BITS), MIN_NORMAL_BITS, cand)
        tb = _key_to_f32(cand).astype(BF16)

        def body(c, acc):
            c0 = pl.multiple_of(c * rows, rows)
            ge = jnp.where(hb_ref[pl.ds(c0, rows), :] >= tb, one, zero)
            return acc + _tree_sum(ge.reshape(rows // 16, 16, TQ)).astype(F32)

        acc = lax.fori_loop(0, n_chunks, body, jnp.zeros((16, TQ), F32))
        return jnp.sum(acc, axis=0, keepdims=True).astype(I32)

    def count32(pred):
        def body(c, acc):
            c0 = pl.multiple_of(c * rows, rows)
            hit = pred(keys_ref[pl.ds(c0, rows), :], c0).astype(I32)
            return acc + jnp.sum(hit.reshape(rows // 8, 8, TQ), axis=0)

        acc = lax.fori_loop(0, n_chunks, body, jnp.zeros((8, TQ), I32))
        return jnp.sum(acc, axis=0, keepdims=True)

    def search(count, bits, thr, c_thr, n_fixed):
        def step(carry):
            b, thr, c_thr = carry
            cand = thr + lax.shift_left(jnp.int32(1), bits[0] - b)
            cnt = count(cand)
            ok = cnt >= top_k
            return b + 1, jnp.where(ok, cand, thr), jnp.where(ok, cnt, c_thr)

        def unresolved(carry):
            b, _, c_thr = carry
            return (b <= bits[0] - bits[1]) & (jnp.max(c_thr) > top_k)

        carry = lax.fori_loop(0, n_fixed, lambda _, c: step(c), (jnp.int32(0), thr, c_thr))
        return lax.while_loop(unresolved, step, carry)[1:]

    cnt = count16(jnp.zeros((1, TQ), I32))
    ok = cnt >= top_k
    thr = jnp.where(ok, 0, INT_MIN).astype(I32)
    c_thr = jnp.where(ok, cnt, 2 ** 30)
    thr, c_thr = search(count16, (30, 16), thr, c_thr, 15)
    thr, c_thr = search(lambda cand: count32(lambda keys, c0: keys >= cand), (15, 0), thr, c_thr, 11)
    thr_ref[...] = jnp.maximum(thr, INT_MIN + 1)

    split = (c_thr > top_k) & (thr > INT_MIN)

    @pl.when(jnp.max(split.astype(I32)) > 0)
    def _break_ties():
        need = top_k - count32(lambda keys, c0: keys > thr)

        def equal_before(limit):
            def pred(keys, c0):
                idx = c0 + lax.broadcasted_iota(I32, keys.shape, 0)
                return (keys == thr) & (idx < limit)
            return count32(pred)

        def step(b, q):
            cand = q + lax.shift_left(jnp.int32(1), idx_bits - 1 - b)
            return jnp.where(equal_before(cand) < need, cand, q)

        q = lax.fori_loop(0, idx_bits, step, jnp.zeros((1, TQ), I32))

        def lower(c, carry):
            c0 = pl.multiple_of(c * rows, rows)
            keys = keys_ref[pl.ds(c0, rows), :]
            idx = c0 + lax.broadcasted_iota(I32, keys.shape, 0)
            drop = split & (keys == thr) & (idx > q)
            keys_ref[pl.ds(c0, rows), :] = jnp.where(drop, thr - 1, keys)
            return carry

        lax.fori_loop(0, n_chunks, lower, 0)


def _dsa_body(top_k, idx_bits, iq_ref, iw_ref, ik_ref, q_ref, k_ref, vt_ref, bias_ref, o_ref,
              keys_ref, hb_ref, thr_ref, m_ref, acc_ref, s_even, s_odd):
    i = pl.program_id(0)
    g = pl.program_id(1)
    n_tiles = i + 1
    n_chunks = (i + KC) // KC

    @pl.when(g == 0)
    def _select():
        w_all = iw_ref[...]
        zpad = jnp.zeros((IDX_DIM, TQ), BF16)
        qz = [jnp.concatenate([iq_ref[h * IDX_DIM:(h + 1) * IDX_DIM, :], zpad], axis=0)
              for h in range(IDX_HEADS)]

        def score_tiles(p, carry):
            for u in range(2):
                k0 = pl.multiple_of((2 * p + u) * TK, TK)
                ikt = ik_ref[pl.ds(k0, TK), :]
                sc = jnp.zeros((TK, TQ), F32)
                for h in range(IDX_HEADS):
                    sc = sc + w_all[h:h + 1, :] * jnp.maximum(_dot(ikt, qz[h]), 0.0)
                keys_ref[pl.ds(k0, TK), :], hb_ref[pl.ds(k0, TK), :] = _score_codes(sc)
            return carry

        lax.fori_loop(0, (n_tiles + 1) // 2, score_tiles, 0)

        d0 = pl.multiple_of(i * TK, TK)
        sk = lax.broadcasted_iota(I32, (TK, TQ), 0) // CHUNK
        tq = lax.broadcasted_iota(I32, (TK, TQ), 1) // CHUNK
        no_key = jnp.full((TK, TQ), INT_MIN, I32)
        no_hb = jnp.full((TK, TQ), jnp.nan, BF16)
        keys_ref[pl.ds(d0, TK), :] = jnp.where(sk <= tq, keys_ref[pl.ds(d0, TK), :], no_key)
        hb_ref[pl.ds(d0, TK), :] = jnp.where(sk <= tq, hb_ref[pl.ds(d0, TK), :], no_hb)

        def fill_tile(j, carry):
            k0 = pl.multiple_of(j * TK, TK)
            keys_ref[pl.ds(k0, TK), :] = no_key
            hb_ref[pl.ds(k0, TK), :] = no_hb
            return carry

        lax.fori_loop(n_tiles, n_chunks * KC, fill_tile, 0)
        _select_topk(keys_ref, hb_ref, thr_ref, n_chunks, top_k, idx_bits)

    _init_state(m_ref, acc_ref)
    q_halves = _half_masked(q_ref[...])
    thr = thr_ref[...]

    def selected(k0):
        return keys_ref[pl.ds(k0, TK), :] >= thr

    def consume(c, s_ref):
        for a in range(2):
            vts = [_with_ones(vt_ref[c * KC + u, a * HEAD_DIM:(a + 1) * HEAD_DIM, :])
                   for u in range(KC)]
            _softmax_step_ref(s_ref, a, vts, m_ref, acc_ref)

    _score_pipeline(i, n_chunks, k_ref, q_halves, bias_ref, (0, 1), selected, consume,
                    s_even, s_odd)
    for a in range(2):
        o_ref[a * HEAD_DIM:(a + 1) * HEAD_DIM, :] = _normalised(acc_ref[a], HEAD_DIM)


def _dsa_attention(iq_t, iw_t, ik, q_t, k, v_t, bias, top_k):
    s = k.shape[0]
    n_pairs = A_HEADS // 2
    return pl.pallas_call(
        functools.partial(_dsa_body, top_k, (s - 1).bit_length()),
        grid=(s // TQ, n_pairs),
        in_specs=[
            pl.BlockSpec((IDX_HEADS * IDX_DIM, TQ), lambda i, g: (0, i)),
            pl.BlockSpec((IDX_HEADS, TQ), lambda i, g: (0, i)),
            pl.BlockSpec((s, 128), lambda i, g: (0, 0)),
            pl.BlockSpec((128, TQ), lambda i, g: (g, i)),
            pl.BlockSpec((s, 128), lambda i, g: (0, g)),
            pl.BlockSpec((s // TK, 128, TK), lambda i, g: (0, g, 0)),
            pl.BlockSpec((2, 4, TK, TQ), lambda i, g: (g, 0, 0, 0)),
        ],
        out_specs=pl.BlockSpec((128, TQ), lambda i, g: (g, i)),
        out_shape=jax.ShapeDtypeStruct((A_HEADS * HEAD_DIM, s), F32),
        scratch_shapes=[
            pltpu.VMEM((s, TQ), I32),
            pltpu.VMEM((s, TQ), BF16),
            pltpu.VMEM((1, TQ), I32),
            pltpu.VMEM((2, 1, TQ), F32),
            pltpu.VMEM((2, HEAD_DIM + ROWSUM_ROWS, TQ), F32),
            pltpu.VMEM((2, KC * TK, TQ), F32),
            pltpu.VMEM((2, KC * TK, TQ), F32),
        ],
        compiler_params=_cparams(2),
        name="dsa_attention",
    )(iq_t, iw_t, ik, q_t, k, v_t, bias)


def _diff_body(lambda_init, q_ref, k_ref, vt_ref, bias_ref, lam_ref, subln_ref, o_ref,
               m_ref, acc_ref, s_even, s_odd):
    i = pl.program_id(0)
    _init_state(m_ref, acc_ref)
    q_halves = _half_masked(q_ref[...])

    def consume(c, s_ref):
        for hd in range(DIFF_HEADS):
            rows = slice(hd * B_VDIM, (hd + 1) * B_VDIM)
            vts = [_with_ones(vt_ref[c * KC + u, rows, :]) for u in range(KC)]
            for m in range(2):
                _softmax_step_ref(s_ref, 2 * hd + m, vts, m_ref, acc_ref)

    heads = tuple(hd for hd in range(DIFF_HEADS) for _ in range(2))
    _score_pipeline(i, (i + KC) // KC, k_ref, q_halves, bias_ref, heads, None, consume,
                    s_even, s_odd)

    lq1, lk1, lq2, lk2 = (lam_ref[r:r + 1, :] for r in range(4))
    lam = (jnp.exp(jnp.sum(lq1 * lk1, keepdims=True)) - jnp.exp(jnp.sum(lq2 * lk2, keepdims=True))
           + lambda_init)
    for hd in range(DIFF_HEADS):
        o = (_normalised(acc_ref[2 * hd], B_VDIM)
             - lam * _normalised(acc_ref[2 * hd + 1], B_VDIM))
        ms = jnp.mean(o * o, axis=0, keepdims=True)
        o_ref[hd * B_VDIM:(hd + 1) * B_VDIM, :] = (o * lax.rsqrt(ms + EPS) * subln_ref[...]
                                                   * (1.0 - lambda_init))


def _diff_attention(q_t, k, v_t, bias, lam_rows, subln, lambda_init):
    s = k.shape[0]
    f = DIFF_HEADS * B_VDIM
    n_streams = 2 * DIFF_HEADS
    return pl.pallas_call(
        functools.partial(_diff_body, lambda_init),
        grid=(s // TQ, B_HEADS // DIFF_HEADS),
        in_specs=[
            pl.BlockSpec((f, TQ), lambda i, h: (h, i)),
            pl.BlockSpec((s, f), lambda i, h: (0, h)),
            pl.BlockSpec((s // TK, f, TK), lambda i, h: (0, h, 0)),
            pl.BlockSpec((DIFF_HEADS, 4, TK, TQ), lambda i, h: (h, 0, 0, 0)),
            pl.BlockSpec((4, HEAD_DIM), lambda i, h: (0, 0)),
            pl.BlockSpec((B_VDIM, 1), lambda i, h: (0, 0)),
        ],
        out_specs=pl.BlockSpec((f, TQ), lambda i, h: (h, i)),
        out_shape=jax.ShapeDtypeStruct((B_HEADS * B_VDIM, s), F32),
        scratch_shapes=[
            pltpu.VMEM((n_streams, 1, TQ), F32),
            pltpu.VMEM((n_streams, B_VDIM + ROWSUM_ROWS, TQ), F32),
            pltpu.VMEM((n_streams, KC * TK, TQ), F32),
            pltpu.VMEM((n_streams, KC * TK, TQ), F32),
        ],
        compiler_params=_cparams(2),
        name="diff_attention",
    )(q_t, k, v_t, bias, lam_rows, subln.astype(F32).reshape(B_VDIM, 1))


def _band_bias_tiles(rel_bias):
    table = rel_bias.astype(F32) * LOG2E
    r = jnp.arange(-(BAND_T - 1), BAND_T, dtype=jnp.int32)
    u = jnp.stack([table[jnp.clip((1 - blk) * BAND_T + r, -REL_CLIP, REL_CLIP) + REL_CLIP].T
                   for blk in range(2)], axis=1)

    def in_band(blk):
        def mask(sk, tq):
            kc = sk // CHUNK + blk * (BAND_T // CHUNK)
            qc = tq // CHUNK + BAND_T // CHUNK
            return (kc <= qc) & (kc >= qc - (C_BAND - 1))
        return mask

    return _toeplitz_tiles(u, BAND_T, BAND_T, (in_band(0), in_band(1)))


def _band_body(q_ref, kp_ref, kc_ref, vp_ref, vc_ref, bias_ref, o_ref):
    i = pl.program_id(1)
    q_halves = _half_masked(q_ref[...])
    for a in range(len(q_halves)):
        rows = slice(a * HEAD_DIM, (a + 1) * HEAD_DIM)
        s_prev = _dot(kp_ref[...], q_halves[a]) + jnp.where(i > 0, bias_ref[a, 0], NEG)
        s_cur = _dot(kc_ref[...], q_halves[a]) + bias_ref[a, 1]
        m = jnp.maximum(jnp.max(s_prev, axis=0, keepdims=True),
                        jnp.max(s_cur, axis=0, keepdims=True))
        acc = (_dot(_with_ones(vp_ref[0, rows, :]), jnp.exp2(s_prev - m).astype(BF16))
               + _dot(_with_ones(vc_ref[0, rows, :]), jnp.exp2(s_cur - m).astype(BF16)))
        o_ref[rows, :] = _normalised(acc, HEAD_DIM)


def _band_attention(q_t, k, v_t, bias):
    s = k.shape[0]
    t = BAND_T
    f = BAND_HEADS * HEAD_DIM
    prev = lambda i: jnp.maximum(i - 1, 0)
    return pl.pallas_call(
        _band_body,
        grid=(C_HEADS // BAND_HEADS, s // t),
        in_specs=[
            pl.BlockSpec((f, t), lambda p, i: (p, i)),
            pl.BlockSpec((t, f), lambda p, i: (prev(i), p)),
            pl.BlockSpec((t, f), lambda p, i: (i, p)),
            pl.BlockSpec((1, f, t), lambda p, i: (prev(i), p, 0)),
            pl.BlockSpec((1, f, t), lambda p, i: (i, p, 0)),
            pl.BlockSpec((BAND_HEADS, 2, t, t), lambda p, i: (p, 0, 0, 0)),
        ],
        out_specs=pl.BlockSpec((f, t), lambda p, i: (p, i)),
        out_shape=jax.ShapeDtypeStruct((C_HEADS * HEAD_DIM, s), F32),
        compiler_params=_cparams(2),
        name="band_attention",
    )(q_t, k, k, v_t, v_t, bias)


def _even_mixer(h, mix_g, w_in, a_qg, a_kg, idx_kg, b_qg, b_kg, lq1, lk1, lq2, lk2, b_subln,
                w_out, t5_bias, layer_idx):
    s = h.shape[0]
    sizes = [A_HEADS * HEAD_DIM] * 3 + [IDX_HEADS * IDX_DIM, IDX_DIM, IDX_HEADS] + \
            [B_HEADS * 2 * HEAD_DIM] * 2 + [B_HEADS * B_VDIM]
    cuts = np.cumsum([0] + sizes)
    w = [w_in[:, cuts[n]:cuts[n + 1]] for n in range(len(sizes))]
    qk_scale = HEAD_DIM ** -0.5 * LOG2E
    segs = [
        _Seg(w[0], HEAD_DIM, a_qg, qk_scale, "fm"),
        _Seg(w[1], HEAD_DIM, a_kg, 1.0, "tm"),
        _Seg(w[2], layout="vt"),
        _Seg(w[3], scale=IDX_DIM ** -0.5, layout="fm"),
        _Seg(w[4], IDX_DIM, idx_kg, 1.0, "tm", pad=IDX_DIM),
        _Seg(w[5], scale=IDX_HEADS ** -0.5, layout="fm", dtype=F32),
        _Seg(w[6], HEAD_DIM, b_qg, qk_scale, "fm"),
        _Seg(w[7], HEAD_DIM, b_kg, 1.0, "tm"),
        _Seg(w[8], layout="vt"),
    ]
    aq_t, ak, av_t, iq_t, ik, iw_t, bq_t, bk, bv_t = _project(h, mix_g, segs)
    bias = _t5_bias_tiles(t5_bias)
    assert s % (KC * TK) == 0
    top_k = min(TOPK_MAX, s // 4)
    out_a = _dsa_attention(iq_t, iw_t, ik, aq_t, ak, av_t, bias[:A_HEADS], top_k)
    lambda_init = 0.8 - 0.6 * math.exp(-0.3 * layer_idx)
    lam_rows = jnp.stack([lq1, lk1, lq2, lk2]).astype(F32)
    out_b = _diff_attention(bq_t, bk, bv_t, bias[A_HEADS:], lam_rows, b_subln, lambda_init)
    na = A_HEADS * HEAD_DIM
    return [out_a, out_b], [w_out[:na], w_out[na:]]


def _odd_mixer(h, mix_g, w_in, c_qg, c_kg, rel_bias, w_out):
    f = C_HEADS * HEAD_DIM
    segs = [
        _Seg(w_in[:, :f], HEAD_DIM, c_qg, HEAD_DIM ** -0.5 * LOG2E, "fm"),
        _Seg(w_in[:, f:2 * f], HEAD_DIM, c_kg, 1.0, "tm"),
        _Seg(w_in[:, 2 * f:], layout="vt", tile=BAND_T),
    ]
    q_t, k, v_t = _project(h, mix_g, segs)
    out = _band_attention(q_t, k, v_t, _band_bias_tiles(rel_bias))
    return [out], [w_out]


def _mixer_out_and_memory(h, mixer_out, mem, mg, sg, wq, wkv, qg, kg, wo):
    ots, ws = mixer_out
    f = M_HEADS * M_DIM
    k, v_t = _project(mem, sg, [_Seg(wkv[:, :f], M_DIM, kg, 1.0, "tm"),
                                _Seg(wkv[:, f:], layout="vt", tile=mem.shape[0])])
    return _mix_out_mem(h, ots, ws, mg, wq, qg, k, v_t, wo)


def kernel(x, mem, t5_bias,
           l0_ffn1_norm, l0_ffn1_wg, l0_ffn1_wu, l0_ffn1_wd,
           l0_mix_norm, l0_w_in, l0_a_q_norm, l0_a_k_norm, l0_idx_k_norm,
           l0_b_q_norm, l0_b_k_norm, l0_b_lq1, l0_b_lk1, l0_b_lq2, l0_b_lk2, l0_b_subln, l0_w_out,
           l0_mem_norm, l0_mem_src_norm, l0_mem_wq, l0_mem_wkv, l0_mem_q_norm, l0_mem_k_norm, l0_mem_wo,
           l0_ffn2_norm, l0_ffn2_wg, l0_ffn2_wu, l0_ffn2_wd,
           l1_ffn1_norm, l1_ffn1_wg, l1_ffn1_wu, l1_ffn1_wd,
           l1_mix_norm, l1_w_in, l1_c_q_norm, l1_c_k_norm, l1_c_rel_bias, l1_w_out,
           l1_mem_norm, l1_mem_src_norm, l1_mem_wq, l1_mem_wkv, l1_mem_q_norm, l1_mem_k_norm, l1_mem_wo,
           l1_ffn2_norm, l1_ffn2_wg, l1_ffn2_wu, l1_ffn2_wd):
    bsz, seq, d = x.shape
    assert bsz == 1 and mem.shape[0] == 1
    h = x.reshape(seq, d)
    m = mem.reshape(mem.shape[1], d)

    h = _ffn(h, l0_ffn1_norm, l0_ffn1_wg, l0_ffn1_wu, l0_ffn1_wd)
    mixed = _even_mixer(h, l0_mix_norm, l0_w_in, l0_a_q_norm, l0_a_k_norm, l0_idx_k_norm,
                        l0_b_q_norm, l0_b_k_norm, l0_b_lq1, l0_b_lk1, l0_b_lq2, l0_b_lk2,
                        l0_b_subln, l0_w_out, t5_bias, 0)
    h = _mixer_out_and_memory(h, mixed, m, l0_mem_norm, l0_mem_src_norm, l0_mem_wq, l0_mem_wkv,
                              l0_mem_q_norm, l0_mem_k_norm, l0_mem_wo)
    h = _ffn(h, l0_ffn2_norm, l0_ffn2_wg, l0_ffn2_wu, l0_ffn2_wd)

    h = _ffn(h, l1_ffn1_norm, l1_ffn1_wg, l1_ffn1_wu, l1_ffn1_wd)
    mixed = _odd_mixer(h, l1_mix_norm, l1_w_in, l1_c_q_norm, l1_c_k_norm, l1_c_rel_bias, l1_w_out)
    h = _mixer_out_and_memory(h, mixed, m, l1_mem_norm, l1_mem_src_norm, l1_mem_wq, l1_mem_wkv,
                              l1_mem_q_norm, l1_mem_k_norm, l1_mem_wo)
    h = _ffn(h, l1_ffn2_norm, l1_ffn2_wg, l1_ffn2_wu, l1_ffn2_wd)
    return h.reshape(bsz, seq, d)
```
